```python
import jax
import jax.numpy as jnp
from jax import lax
import numpy as np

D_MODEL = 1024
BATCH = 4
SEQ = 4096
DEPTH = 2

ML_HEADS = 4
ML_DIM = 128
ML_WIDTH = ML_HEADS * ML_DIM
ML_CHUNK = 64
CONV_W = 4
AT_HEADS = 8
AT_KV_HEADS = 2
AT_DIM = 64
AT_WIDTH = AT_HEADS * AT_DIM
KV_WIDTH = AT_KV_HEADS * AT_DIM
IDX_HEADS = 4
IDX_DIM = 64
TOPK_MAX = 256
Q_BLOCK = 128
ROPE_THETA = 500000.0
D_FF_DENSE = 2816
N_EXPERTS = 8
TOP_K_EXPERTS = 2
D_FF_EXPERT = 3584
ALPHA = (2 * DEPTH) ** 0.25
BETA = (8 * DEPTH) ** -0.25
LN_EPS = 1e-5
N_DENSE = (DEPTH + 1) // 2
N_MOE = DEPTH // 2
PROJ_SIZES = (ML_WIDTH, ML_WIDTH, ML_WIDTH, ML_WIDTH, ML_HEADS, ML_HEADS,
              AT_WIDTH, KV_WIDTH, KV_WIDTH, IDX_HEADS * IDX_DIM, IDX_DIM, IDX_HEADS)
D_IN_PROJ = sum(PROJ_SIZES)

kernel_name = "hybrid_mlstm_dsa_moe_deepnorm"


def layer_norm(x, g, b):
    xf = x.astype(jnp.float32)
    mu = jnp.mean(xf, -1, keepdims=True)
    var = jnp.mean(jnp.square(xf - mu), -1, keepdims=True)
    y = (xf - mu) * lax.rsqrt(var + LN_EPS) * g.astype(jnp.float32) + b.astype(jnp.float32)
    return y.astype(x.dtype)


def head_norm(h, g):
    mu = jnp.mean(h, -1, keepdims=True)
    var = jnp.mean(jnp.square(h - mu), -1, keepdims=True)
    return (h - mu) * lax.rsqrt(var + LN_EPS) * g.astype(jnp.float32)


def partial_rope(x, pos):
    rot = x.shape[-1] // 4
    half = rot // 2
    inv = ROPE_THETA ** (-jnp.arange(half, dtype=jnp.float32) / half)
    ang = pos.astype(jnp.float32)[:, None] * inv[None, :]
    cos = jnp.cos(ang)[:, None, :]
    sin = jnp.sin(ang)[:, None, :]
    xf = x.astype(jnp.float32)
    x1 = xf[..., :half]
    x2 = xf[..., half:rot]
    out = jnp.concatenate([x1 * cos - x2 * sin, x2 * cos + x1 * sin, xf[..., rot:]], -1)
    return out.astype(x.dtype)


def causal_conv(x, w, b):
    s = x.shape[1]
    xp = jnp.pad(x, ((0, 0), (CONV_W - 1, 0), (0, 0)))
    y = xp[:, 0:s] * w[0]
    for j in range(1, CONV_W):
        y = y + xp[:, j:j + s] * w[j]
    return y + b


def mlstm_chunkwise(q, k, v, i_pre, f_pre):
    bsz, s, nh, d = q.shape
    nc = s // ML_CHUNK

    def chunks(a):
        a = a.astype(jnp.float32).reshape((bsz, nc, ML_CHUNK) + a.shape[2:])
        return jnp.moveaxis(jnp.moveaxis(a, 1, 0), 3, 2)

    qc = chunks(q) * (d ** -0.5)
    kc = chunks(k)
    vc = chunks(v)
    lf = jax.nn.log_sigmoid(chunks(f_pre))
    li = chunks(i_pre)
    bcum = jnp.cumsum(lf, axis=-1)
    causal = jnp.tril(jnp.ones((ML_CHUNK, ML_CHUNK), dtype=bool))

    def step(carry, inp):
        c_st, n_st, m_st = carry
        qj, kj, vj, bj, lij = inp
        dmat = bj[..., :, None] - bj[..., None, :] + lij[..., None, :]
        dmat = jnp.where(causal, dmat, -jnp.inf)
        g_inter = bj + m_st[..., None]
        m_t = jnp.maximum(g_inter, jnp.max(dmat, -1))
        w_inter = jnp.exp(g_inter - m_t)
        sc = jnp.einsum("bhtd,bhsd->bhts", qj, kj) * jnp.exp(dmat - m_t[..., None])
        num = w_inter[..., None] * jnp.einsum("bhtd,bhde->bhte", qj, c_st) + jnp.einsum("bhts,bhse->bhte", sc, vj)
        den = w_inter * jnp.einsum("bhtd,bhd->bht", qj, n_st) + jnp.sum(sc, -1)
        h = num / jnp.maximum(jnp.abs(den), jnp.exp(-m_t))[..., None]
        b_last = bj[..., -1]
        g_state = b_last + m_st
        ls = b_last[..., None] - bj + lij
        m_new = jnp.maximum(g_state, jnp.max(ls, -1))
        ws = jnp.exp(ls - m_new[..., None])
        decay = jnp.exp(g_state - m_new)
        c_new = decay[..., None, None] * c_st + jnp.einsum("bhs,bhsd,bhse->bhde", ws, kj, vj)
        n_new = decay[..., None] * n_st + jnp.einsum("bhs,bhsd->bhd", ws, kj)
        return (c_new, n_new, m_new), h

    init = (jnp.zeros((bsz, nh, d, d), jnp.float32),
            jnp.zeros((bsz, nh, d), jnp.float32),
            jnp.zeros((bsz, nh), jnp.float32))
    _, h = lax.scan(step, init, (qc, kc, vc, bcum, li))
    return jnp.transpose(h, (1, 0, 3, 2, 4)).reshape(bsz, s, nh, d)


def dsa_attention(q, k, v, qi, ki, wi):
    bsz, s = q.shape[0], q.shape[1]
    topk = min(TOPK_MAX, s // 4)
    nb = s // Q_BLOCK
    rep = AT_HEADS // AT_KV_HEADS
    key_pos = jnp.arange(s)
    kif = ki.astype(jnp.float32)

    def block(j):
        t0 = j * Q_BLOCK
        qb = lax.dynamic_slice_in_dim(q, t0, Q_BLOCK, axis=1)
        qib = lax.dynamic_slice_in_dim(qi, t0, Q_BLOCK, axis=1).astype(jnp.float32)
        wib = lax.dynamic_slice_in_dim(wi, t0, Q_BLOCK, axis=1).astype(jnp.float32) * (IDX_HEADS ** -0.5)
        tpos = t0 + jnp.arange(Q_BLOCK)
        sc = jax.nn.relu(jnp.einsum("bthd,bsd->bths", qib, kif) * (IDX_DIM ** -0.5))
        iscore = jnp.einsum("bth,bths->bts", wib, sc)
        allowed = key_pos[None, :] <= tpos[:, None]
        iscore = jnp.where(allowed[None], iscore, -jnp.inf)
        _, idx = lax.top_k(iscore, topk)
        valid = idx <= tpos[None, :, None]
        kg = jax.vmap(lambda kb, ib: kb[ib])(k, idx).astype(jnp.float32)
        vg = jax.vmap(lambda vb, ib: vb[ib])(v, idx).astype(jnp.float32)
        qg = qb.astype(jnp.float32).reshape(bsz, Q_BLOCK, AT_KV_HEADS, rep, AT_DIM)
        logits = jnp.einsum("btgrd,btkgd->btgrk", qg, kg) * (AT_DIM ** -0.5)
        logits = jnp.where(valid[:, :, None, None, :], logits, -jnp.inf)
        p = jax.nn.softmax(logits, axis=-1)
        o = jnp.einsum("btgrk,btkgd->btgrd", p, vg)
        return o.reshape(bsz, Q_BLOCK, AT_WIDTH).astype(q.dtype)

    out = lax.map(block, jnp.arange(nb))
    return jnp.transpose(out, (1, 0, 2, 3)).reshape(bsz, s, AT_WIDTH)


def swiglu(x, wg, wu, wd):
    return (jax.nn.silu(x @ wg) * (x @ wu)) @ wd


def moe_ffn(x, w_router, b_router, w_gate, w_up, w_down):
    bsz, s, dm = x.shape
    xt = x.reshape(-1, dm)
    logits = xt.astype(jnp.float32) @ w_router.astype(jnp.float32) + b_router.astype(jnp.float32)
    top_val, top_idx = lax.top_k(logits, TOP_K_EXPERTS)
    gates = jax.nn.softmax(top_val, axis=-1)
    combine = jnp.sum(jax.nn.one_hot(top_idx, N_EXPERTS, dtype=jnp.float32) * gates[..., None], axis=1)
    y = jnp.zeros((xt.shape[0], dm), jnp.float32)
    for e in range(N_EXPERTS):
        y = y + combine[:, e:e + 1] * swiglu(xt, w_gate[e], w_up[e], w_down[e]).astype(jnp.float32)
    return y.reshape(bsz, s, dm).astype(x.dtype)


def setup_inputs(seed: int = 0) -> dict:
    key = jax.random.key(seed)
    ks = jax.random.split(key, 24)

    def nrm(k, shape, scale):
        return jax.random.normal(k, shape, jnp.float32) * scale

    return {
        "x": nrm(ks[0], (BATCH, SEQ, D_MODEL), 1.0),
        "w_in": nrm(ks[1], (DEPTH, D_MODEL, D_IN_PROJ), D_MODEL ** -0.5),
        "ml_conv_w": nrm(ks[2], (DEPTH, CONV_W, 2 * ML_WIDTH), CONV_W ** -0.5),
        "ml_conv_b": nrm(ks[3], (DEPTH, 2 * ML_WIDTH), 0.02),
        "ml_i_b": nrm(ks[4], (DEPTH, ML_HEADS), 0.1),
        "ml_f_b": 3.0 + jnp.linspace(0.0, 3.0, ML_HEADS, dtype=jnp.float32)[None, :] + nrm(ks[5], (DEPTH, ML_HEADS), 0.1),
        "ml_norm_g": 1.0 + nrm(ks[6], (DEPTH, ML_WIDTH), 0.02),
        "idx_k_norm_g": 1.0 + nrm(ks[7], (DEPTH, IDX_DIM), 0.02),
        "idx_k_norm_b": nrm(ks[8], (DEPTH, IDX_DIM), 0.02),
        "w_out": nrm(ks[9], (DEPTH, D_MODEL, D_MODEL), BETA * D_MODEL ** -0.5),
        "ln1_g": 1.0 + nrm(ks[10], (DEPTH, D_MODEL), 0.02),
        "ln1_b": nrm(ks[11], (DEPTH, D_MODEL), 0.02),
        "ln2_g": 1.0 + nrm(ks[12], (DEPTH, D_MODEL), 0.02),
        "ln2_b": nrm(ks[13], (DEPTH, D_MODEL), 0.02),
        "ffn_w_gate": nrm(ks[14], (N_DENSE, D_MODEL, D_FF_DENSE), D_MODEL ** -0.5),
        "ffn_w_up": nrm(ks[15], (N_DENSE, D_MODEL, D_FF_DENSE), D_MODEL ** -0.5),
        "ffn_w_down": nrm(ks[16], (N_DENSE, D_FF_DENSE, D_MODEL), BETA * D_FF_DENSE ** -0.5),
        "moe_w_router": nrm(ks[17], (N_MOE, D_MODEL, N_EXPERTS), D_MODEL ** -0.5),
        "moe_b_router": nrm(ks[18], (N_MOE, N_EXPERTS), 0.01),
        "moe_w_gate": nrm(ks[19], (N_MOE, N_EXPERTS, D_MODEL, D_FF_EXPERT), D_MODEL ** -0.5),
        "moe_w_up": nrm(ks[20], (N_MOE, N_EXPERTS, D_MODEL, D_FF_EXPERT), D_MODEL ** -0.5),
        "moe_w_down": nrm(ks[21], (N_MOE, N_EXPERTS, D_FF_EXPERT, D_MODEL), BETA * D_FF_EXPERT ** -0.5),
    }


def reference(x, w_in, ml_conv_w, ml_conv_b, ml_i_b, ml_f_b, ml_norm_g, idx_k_norm_g, idx_k_norm_b,
              w_out, ln1_g, ln1_b, ln2_g, ln2_b, ffn_w_gate, ffn_w_up, ffn_w_down,
              moe_w_router, moe_b_router, moe_w_gate, moe_w_up, moe_w_down):
    bsz, s, _ = x.shape
    pos = jnp.arange(s)
    split_at = np.cumsum(np.array(PROJ_SIZES))[:-1].tolist()
    for l in range(DEPTH):
        proj = x @ w_in[l]
        mq, mk, mv, mo, mi, mf, aq, ak, av, iq, ik, iw = jnp.split(proj, split_at, axis=-1)

        qk = jax.nn.silu(causal_conv(jnp.concatenate([mq, mk], -1), ml_conv_w[l], ml_conv_b[l]))
        mq_c, mk_c = jnp.split(qk, 2, axis=-1)
        hm = mlstm_chunkwise(mq_c.reshape(bsz, s, ML_HEADS, ML_DIM),
                             mk_c.reshape(bsz, s, ML_HEADS, ML_DIM),
                             mv.reshape(bsz, s, ML_HEADS, ML_DIM),
                             mi + ml_i_b[l], mf + ml_f_b[l])
        hm = head_norm(hm, ml_norm_g[l].reshape(ML_HEADS, ML_DIM)).reshape(bsz, s, ML_WIDTH)
        ml_out = (jax.nn.sigmoid(mo.astype(jnp.float32)) * hm).astype(x.dtype)

        aq_r = partial_rope(aq.reshape(bsz, s, AT_HEADS, AT_DIM), pos)
        ak_r = partial_rope(ak.reshape(bsz, s, AT_KV_HEADS, AT_DIM), pos)
        av_h = av.reshape(bsz, s, AT_KV_HEADS, AT_DIM)
        iq_r = partial_rope(iq.reshape(bsz, s, IDX_HEADS, IDX_DIM), pos)
        ik_r = partial_rope(layer_norm(ik, idx_k_norm_g[l], idx_k_norm_b[l])[:, :, None, :], pos)[:, :, 0, :]
        at_out = dsa_attention(aq_r, ak_r, av_h, iq_r, ik_r, iw)

        mix = jnp.concatenate([ml_out, at_out], -1) @ w_out[l]
        x = layer_norm(ALPHA * x + mix, ln1_g[l], ln1_b[l])

        if l % 2 == 0:
            j = l // 2
            ffn = swiglu(x, ffn_w_gate[j], ffn_w_up[j], ffn_w_down[j])
        else:
            j = l // 2
            ffn = moe_ffn(x, moe_w_router[j], moe_b_router[j], moe_w_gate[j], moe_w_up[j], moe_w_down[j])
        x = layer_norm(ALPHA * x + ffn, ln2_g[l], ln2_b[l])
    return x
```

```python
import functools

import jax
import jax.numpy as jnp
import numpy as np
from jax import lax
from jax.experimental import pallas as pl
from jax.experimental.pallas import tpu as pltpu

F32 = jnp.float32
BF16 = jnp.bfloat16
I32 = jnp.int32

ML_HEADS = 4
ML_DIM = 128
ML_WIDTH = ML_HEADS * ML_DIM
CONV_W = 4
AT_HEADS = 8
AT_KV_HEADS = 2
AT_DIM = 64
AT_WIDTH = AT_HEADS * AT_DIM
KV_WIDTH = AT_KV_HEADS * AT_DIM
IDX_HEADS = 4
IDX_DIM = 64
TOPK_MAX = 256
ROPE_THETA = 500000.0
N_EXPERTS = 8
LN_EPS = 1e-5

LANES = 128
SUBLANES = 8
VMEM_LIMIT_BYTES = 56 * 1024 * 1024

COL_MQ = 0
COL_MK = 512
COL_MV = 1024
COL_MO = 1536
COL_AQ = 2048
COL_AKV = 2560
COL_IQ = 2816
COL_SMALL = 3072
PROJ_PAD = 3200
SM_IW = 64
SM_MI = 68
SM_MF = 72

NEG_BIG = -1e30
NT_DIMS = (((1,), (1,)), ((), ()))
TN_DIMS = (((0,), (0,)), ((), ()))


def _cparams(n_axes):
    return pltpu.CompilerParams(dimension_semantics=("arbitrary",) * n_axes,
                                vmem_limit_bytes=VMEM_LIMIT_BYTES)


def _dot(a, b):
    return jnp.dot(a, b, preferred_element_type=F32)


def _layer_norm(z, g, b):
    mu = jnp.mean(z, axis=-1, keepdims=True)
    d = z - mu
    var = jnp.mean(d * d, axis=-1, keepdims=True)
    return d * lax.rsqrt(var + LN_EPS) * g + b


def _silu(x):
    return x / (1.0 + jnp.exp(-x))


def _in_proj_kernel(x_ref, w_ref, o_ref):
    o_ref[...] = _dot(x_ref[...].astype(BF16), w_ref[...])


def _in_proj(x2d, w_bf):
    n, d = x2d.shape
    pw = w_bf.shape[1]
    tm = 512
    return pl.pallas_call(
        _in_proj_kernel,
        grid=(n // tm,),
        in_specs=[pl.BlockSpec((tm, d), lambda i: (i, 0)),
                  pl.BlockSpec((d, pw), lambda i: (0, 0))],
        out_specs=pl.BlockSpec((tm, pw), lambda i: (i, 0)),
        out_shape=jax.ShapeDtypeStruct((n, pw), F32),
        compiler_params=_cparams(1),
        name="in_proj",
    )(x2d, w_bf)


def _mlstm_kernel(q_ref, k_ref, v_ref, o_ref, sm_ref, cw_ref, cb_ref, gb_ref, ng_ref, out_ref,
                  c_st, n_st, m_st, tail):
    chunk = q_ref.shape[0]
    ci = pl.program_id(1)

    @pl.when(ci == 0)
    def _():
        c_st[...] = jnp.zeros_like(c_st)
        n_st[...] = jnp.zeros_like(n_st)
        m_st[...] = jnp.zeros_like(m_st)
        tail[...] = jnp.zeros_like(tail)

    def conv_silu(x, prev, w, b):
        cat = jnp.concatenate([prev, x], axis=0)
        y = pltpu.roll(cat, 3, 0)[SUBLANES:] * w[0:1]
        y = y + pltpu.roll(cat, 2, 0)[SUBLANES:] * w[1:2]
        y = y + pltpu.roll(cat, 1, 0)[SUBLANES:] * w[2:3]
        y = y + x * w[3:4]
        return _silu(y + b)

    xq = q_ref[...]
    xk = k_ref[...]
    cw = cw_ref[...]
    cb = cb_ref[...]
    qs = conv_silu(xq, tail[:, :ML_WIDTH], cw[:, :ML_WIDTH], cb[:, :ML_WIDTH]) * (ML_DIM ** -0.5)
    ks = conv_silu(xk, tail[:, ML_WIDTH:], cw[:, ML_WIDTH:], cb[:, ML_WIDTH:])
    tail[:, :ML_WIDTH] = xq[chunk - SUBLANES:]
    tail[:, ML_WIDTH:] = xk[chunk - SUBLANES:]

    gates = sm_ref[...] + gb_ref[...]
    logf = jnp.minimum(gates, 0.0) - jnp.log1p(jnp.exp(-jnp.abs(gates)))
    row = lax.broadcasted_iota(I32, (chunk, chunk), 0)
    col = lax.broadcasted_iota(I32, (chunk, chunk), 1)
    causal = row >= col
    tri = jnp.where(causal, 1.0, 0.0).astype(BF16)
    hi = logf.astype(BF16)
    r1 = logf - hi.astype(F32)
    mid = r1.astype(BF16)
    lo = (r1 - mid.astype(F32)).astype(BF16)
    bcum = _dot(tri, hi) + _dot(tri, mid) + _dot(tri, lo)
    gates_t = gates.T
    bcum_t = bcum.T

    for h in range(ML_HEADS):
        sl = slice(h * ML_DIM, (h + 1) * ML_DIM)
        qh = qs[:, sl]
        kh = ks[:, sl]
        vb = v_ref[:, sl].astype(BF16)
        b = bcum[:, SM_MF + h:SM_MF + h + 1]
        li = gates[:, SM_MI + h:SM_MI + h + 1]
        r = gates_t[SM_MI + h:SM_MI + h + 1, :] - bcum_t[SM_MF + h:SM_MF + h + 1, :]
        m_prev = m_st[h][0:1, 0:1]
        dmat = jnp.where(causal, b + r, -jnp.inf)
        g_inter = b + m_prev
        m_t = jnp.maximum(g_inter, jnp.max(dmat, axis=1, keepdims=True))
        w_inter = jnp.exp(g_inter - m_t)
        qb = qh.astype(BF16)
        kb = kh.astype(BF16)
        sc = lax.dot_general(qb, kb, NT_DIMS, preferred_element_type=F32) * jnp.exp(dmat - m_t)
        c_old = c_st[h]
        n_old = n_st[h][0:1, :]
        num = w_inter * _dot(qb, c_old.astype(BF16)) + _dot(sc.astype(BF16), vb)
        den = w_inter * jnp.sum(qh * n_old, axis=1, keepdims=True) + jnp.sum(sc, axis=1, keepdims=True)
        hh = num / jnp.maximum(jnp.abs(den), jnp.exp(-m_t))
        b_last = b[chunk - 1:chunk, :]
        g_state = b_last + m_prev
        ls = b_last - b + li
        m_new = jnp.maximum(g_state, jnp.max(ls, axis=0, keepdims=True))
        ws = jnp.exp(ls - m_new)
        decay = jnp.exp(g_state - m_new)
        kw = kh * ws
        c_st[h] = decay * c_old + _dot(kw.T.astype(BF16), vb)
        n_st[h] = jnp.broadcast_to(decay * n_old + jnp.sum(kw, axis=0, keepdims=True), (SUBLANES, ML_DIM))
        m_st[h] = jnp.broadcast_to(m_new, (SUBLANES, LANES))
        mu = jnp.mean(hh, axis=1, keepdims=True)
        d = hh - mu
        var = jnp.mean(d * d, axis=1, keepdims=True)
        hn = d * lax.rsqrt(var + LN_EPS) * ng_ref[:, sl]
        out_ref[:, sl] = (hn / (1.0 + jnp.exp(-o_ref[:, sl]))).astype(out_ref.dtype)


def _mlstm(proj, conv_w, conv_b, gate_b, norm_g, chunk):
    bsz, s, _ = proj.shape
    wblk = ML_WIDTH

    def colspec(col, width):
        return pl.BlockSpec((None, chunk, width), lambda b, c: (b, c, col // width))

    def full2d(shape):
        return pl.BlockSpec(shape, lambda b, c: (0, 0))

    return pl.pallas_call(
        _mlstm_kernel,
        grid=(bsz, s // chunk),
        in_specs=[colspec(COL_MQ, wblk), colspec(COL_MK, wblk), colspec(COL_MV, wblk), colspec(COL_MO, wblk),
                  colspec(COL_SMALL, LANES),
                  full2d((CONV_W, 2 * ML_WIDTH)), full2d((1, 2 * ML_WIDTH)), full2d((1, LANES)),
                  full2d((1, ML_WIDTH))],
        out_specs=pl.BlockSpec((None, chunk, ML_WIDTH), lambda b, c: (b, c, 0)),
        out_shape=jax.ShapeDtypeStruct((bsz, s, ML_WIDTH), BF16),
        scratch_shapes=[pltpu.VMEM((ML_HEADS, ML_DIM, ML_DIM), F32),
                        pltpu.VMEM((ML_HEADS, SUBLANES, ML_DIM), F32),
                        pltpu.VMEM((ML_HEADS, SUBLANES, LANES), F32),
                        pltpu.VMEM((SUBLANES, 2 * ML_WIDTH), F32)],
        compiler_params=_cparams(2),
        name="mlstm",
    )(proj, proj, proj, proj, proj, conv_w, conv_b, gate_b, norm_g)


def _rope(x, cos, sinp, sinm):
    width = x.shape[1]
    reps = width // LANES
    if reps > 1:
        cos = jnp.concatenate([cos] * reps, axis=1)
        sinp = jnp.concatenate([sinp] * reps, axis=1)
        sinm = jnp.concatenate([sinm] * reps, axis=1)
    half = AT_DIM // 8
    return x * cos + pltpu.roll(x, half, 1) * sinp + pltpu.roll(x, width - half, 1) * sinm


def _dsa_prep_kernel(aq_ref, akv_ref, iq_ref, sm_ref, cos_ref, sinp_ref, sinm_ref, lng_ref, lnb_ref,
                     q_out, k_out, vt_out, iq_out, ik_out, iwt_out):
    cos = cos_ref[...]
    sinp = sinp_ref[...]
    sinm = sinm_ref[...]
    q_out[...] = (_rope(aq_ref[...], cos, sinp, sinm) * (AT_DIM ** -0.5)).astype(BF16)
    akv = akv_ref[...]
    k_out[...] = _rope(akv[:, :KV_WIDTH], cos, sinp, sinm).astype(BF16)
    vt_out[...] = akv[:, KV_WIDTH:].T.astype(BF16)
    iq_out[...] = (_rope(iq_ref[...], cos, sinp, sinm) * (IDX_DIM ** -0.5)).astype(BF16)
    sm = sm_ref[...]
    lane = lax.broadcasted_iota(I32, sm.shape, 1)
    is_k = lane < IDX_DIM
    mu = jnp.sum(jnp.where(is_k, sm, 0.0), axis=1, keepdims=True) * (1.0 / IDX_DIM)
    d = jnp.where(is_k, sm - mu, 0.0)
    var = jnp.sum(d * d, axis=1, keepdims=True) * (1.0 / IDX_DIM)
    y = jnp.where(is_k, d * lax.rsqrt(var + LN_EPS) * lng_ref[...] + lnb_ref[...], 0.0)
    ik_out[...] = _rope(y, cos, sinp, sinm).astype(BF16)
    iwt_out[...] = sm.T[SM_IW:SM_IW + SUBLANES, :] * (IDX_HEADS ** -0.5)


def _dsa_prep(proj, cos, sinp, sinm, ln_g, ln_b):
    bsz, s, _ = proj.shape
    t = 512

    def colspec(col, width):
        return pl.BlockSpec((None, t, width), lambda b, c: (b, c, col // width))

    tab = pl.BlockSpec((t, LANES), lambda b, c: (c, 0))
    vec = pl.BlockSpec((1, LANES), lambda b, c: (0, 0))

    def outspec(width):
        return pl.BlockSpec((None, t, width), lambda b, c: (b, c, 0))

    return pl.pallas_call(
        _dsa_prep_kernel,
        grid=(bsz, s // t),
        in_specs=[colspec(COL_AQ, AT_WIDTH), colspec(COL_AKV, 2 * KV_WIDTH), colspec(COL_IQ, IDX_HEADS * IDX_DIM),
                  colspec(COL_SMALL, LANES), tab, tab, tab, vec, vec],
        out_specs=[outspec(AT_WIDTH), outspec(KV_WIDTH),
                   pl.BlockSpec((None, KV_WIDTH, t), lambda b, c: (b, 0, c)),
                   outspec(IDX_HEADS * IDX_DIM), outspec(LANES),
                   pl.BlockSpec((None, SUBLANES, t), lambda b, c: (b, 0, c))],
        out_shape=[jax.ShapeDtypeStruct((bsz, s, AT_WIDTH), BF16),
                   jax.ShapeDtypeStruct((bsz, s, KV_WIDTH), BF16),
                   jax.ShapeDtypeStruct((bsz, KV_WIDTH, s), BF16),
                   jax.ShapeDtypeStruct((bsz, s, IDX_HEADS * IDX_DIM), BF16),
                   jax.ShapeDtypeStruct((bsz, s, LANES), BF16),
                   jax.ShapeDtypeStruct((bsz, SUBLANES, s), F32)],
        compiler_params=_cparams(2),
        name="dsa_prep",
    )(proj, proj, proj, proj, cos, sinp, sinm, ln_g, ln_b)


def _dsa_kernel(q_ref, iq_ref, iwt_ref, k_ref, vt_ref, ik_ref, out_ref, keys, bias, acc, m_s, l_s, x_s,
                *, topk, seq):
    tq = q_ref.shape[0]
    ck = tq
    j = pl.program_id(1)
    nck = j + 1
    t_abs = j * tq + lax.broadcasted_iota(I32, (ck, tq), 1)
    s_loc = lax.broadcasted_iota(I32, (ck, tq), 0)
    int_min = jnp.int32(-2 ** 31)

    iqb = iq_ref[...]
    iwt = iwt_ref[...]
    iq_heads = [iqb[:, h * IDX_DIM:(h + 1) * IDX_DIM] for h in range(IDX_HEADS)]

    def chunk_off(c):
        return pl.multiple_of(c * ck, ck)

    def score_body(c, carry):
        off = chunk_off(c)
        kic = ik_ref[pl.ds(off, ck), :][:, :IDX_DIM]
        s = jnp.zeros((ck, tq), F32)
        for h in range(IDX_HEADS):
            d = lax.dot_general(kic, iq_heads[h], NT_DIMS, preferred_element_type=F32)
            s = s + iwt[h:h + 1, :] * jnp.maximum(d, 0.0)
        s = jnp.where(off + s_loc <= t_abs, s, -jnp.inf)
        s = jnp.where(s == 0.0, 0.0, s)
        bits = pltpu.bitcast(s, I32)
        keys[pl.ds(off, ck), :] = bits ^ ((bits >> 31) & jnp.int32(0x7FFFFFFF))
        return carry

    lax.fori_loop(0, nck, score_body, 0)

    def count(pred):
        def body(c, a):
            off = chunk_off(c)
            m = jnp.where(pred(keys[pl.ds(off, ck), :], off), 1, 0)
            return a + jnp.sum(m.reshape(ck // SUBLANES, SUBLANES, tq), axis=0)
        a = lax.fori_loop(0, nck, body, jnp.zeros((SUBLANES, tq), I32))
        return jnp.sum(a, axis=0, keepdims=True)

    c0 = count(lambda kc, off: kc >= 0)
    thr = jnp.where(c0 >= topk, jnp.int32(0), int_min) + jnp.zeros((1, tq), I32)

    def bit_body(i, thr):
        cand = thr + (jnp.int32(1) << (30 - i))
        c = count(lambda kc, off: kc >= cand)
        return jnp.where(c >= topk, cand, thr)

    thr = lax.fori_loop(0, 31, bit_body, thr)

    n_gt = count(lambda kc, off: kc > thr)
    n_eq = count(lambda kc, off: kc == thr)
    need = topk - n_gt
    excess = n_eq > need
    x_s[...] = jnp.full(x_s.shape, seq, I32)

    @pl.when(jnp.max(jnp.where(excess, 1, 0)) > 0)
    def _():
        x = jnp.zeros((1, tq), I32)
        for bit in reversed(range(max(seq - 1, 1).bit_length())):
            cand = x + (1 << bit)
            c = count(lambda kc, off: (kc == thr) & (off + s_loc < cand))
            x = jnp.where(c < need, cand, x)
        x_s[...] = jnp.broadcast_to(jnp.where(excess, x, seq), x_s.shape)

    xlim = x_s[0:1, :]

    def bias_body(c, carry):
        off = chunk_off(c)
        kc = keys[pl.ds(off, ck), :]
        s_abs = off + s_loc
        sel = (kc > thr) | ((kc == thr) & (s_abs <= xlim))
        bias[pl.ds(off, ck), :] = jnp.where(sel & (s_abs <= t_abs), 0.0, NEG_BIG)
        return carry

    lax.fori_loop(0, nck, bias_body, 0)

    qb = q_ref[...]
    q_heads = [qb[:, h * AT_DIM:(h + 1) * AT_DIM] for h in range(AT_HEADS)]
    m_s[...] = jnp.full(m_s.shape, NEG_BIG, F32)
    l_s[...] = jnp.zeros_like(l_s)
    acc[...] = jnp.zeros_like(acc)
    rep = AT_HEADS // AT_KV_HEADS

    def att_body(c, carry):
        off = chunk_off(c)
        kc = k_ref[pl.ds(off, ck), :]
        vtc = vt_ref[:, pl.ds(off, ck)]
        bc = bias[pl.ds(off, ck), :]
        k_groups = [kc[:, g * AT_DIM:(g + 1) * AT_DIM] for g in range(AT_KV_HEADS)]
        v_groups = [vtc[g * AT_DIM:(g + 1) * AT_DIM, :] for g in range(AT_KV_HEADS)]
        for h in range(AT_HEADS):
            g = h // rep
            rows = slice(h * AT_DIM, (h + 1) * AT_DIM)
            lg = lax.dot_general(k_groups[g], q_heads[h], NT_DIMS, preferred_element_type=F32) + bc
            m_old = m_s[h:h + 1, :]
            m_new = jnp.maximum(m_old, jnp.max(lg, axis=0, keepdims=True))
            a = jnp.exp(m_old - m_new)
            p = jnp.exp(lg - m_new)
            l_s[h:h + 1, :] = a * l_s[h:h + 1, :] + jnp.sum(p, axis=0, keepdims=True)
            acc[rows, :] = a * acc[rows, :] + _dot(v_groups[g], p.astype(BF16))
            m_s[h:h + 1, :] = m_new
        return carry

    lax.fori_loop(0, nck, att_body, 0)

    outs = [acc[h * AT_DIM:(h + 1) * AT_DIM, :] / l_s[h:h + 1, :] for h in range(AT_HEADS)]
    out_ref[...] = jnp.concatenate(outs, axis=0).T.astype(out_ref.dtype)


def _dsa_attention(q_r, k_r, v_t, iq_r, ik_r, iw_t):
    bsz, s, _ = q_r.shape
    tq = 128
    topk = min(TOPK_MAX, s // 4)
    kern = functools.partial(_dsa_kernel, topk=topk, seq=s)
    return pl.pallas_call(
        kern,
        grid=(bsz, s // tq),
        in_specs=[pl.BlockSpec((None, tq, AT_WIDTH), lambda b, j: (b, j, 0)),
                  pl.BlockSpec((None, tq, IDX_HEADS * IDX_DIM), lambda b, j: (b, j, 0)),
                  pl.BlockSpec((None, SUBLANES, tq), lambda b, j: (b, 0, j)),
                  pl.BlockSpec((None, s, KV_WIDTH), lambda b, j: (b, 0, 0)),
                  pl.BlockSpec((None, KV_WIDTH, s), lambda b, j: (b, 0, 0)),
                  pl.BlockSpec((None, s, LANES), lambda b, j: (b, 0, 0))],
        out_specs=pl.BlockSpec((None, tq, AT_WIDTH), lambda b, j: (b, j, 0)),
        out_shape=jax.ShapeDtypeStruct((bsz, s, AT_WIDTH), BF16),
        scratch_shapes=[pltpu.VMEM((s, tq), I32),
                        pltpu.VMEM((s, tq), F32),
                        pltpu.VMEM((AT_WIDTH, tq), F32),
                        pltpu.VMEM((AT_HEADS, tq), F32),
                        pltpu.VMEM((AT_HEADS, tq), F32),
                        pltpu.VMEM((SUBLANES, tq), I32)],
        compiler_params=_cparams(2),
        name="dsa_attn",
    )(q_r, iq_r, iw_t, k_r, v_t, ik_r)


def _out_ln_kernel(ml_ref, at_ref, x_ref, wt_ref, wb_ref, g_ref, b_ref, o_ref, *, alpha):
    mix = _dot(ml_ref[...], wt_ref[...]) + _dot(at_ref[...], wb_ref[...])
    o_ref[...] = _layer_norm(alpha * x_ref[...] + mix, g_ref[...], b_ref[...])


def _out_proj_ln(ml, at, x2d, w_top, w_bot, g, b, alpha):
    n, d = x2d.shape
    tm = 512
    row = lambda i: (i, 0)
    const = lambda i: (0, 0)
    return pl.pallas_call(
        functools.partial(_out_ln_kernel, alpha=alpha),
        grid=(n // tm,),
        in_specs=[pl.BlockSpec((tm, ML_WIDTH), row), pl.BlockSpec((tm, AT_WIDTH), row), pl.BlockSpec((tm, d), row),
                  pl.BlockSpec((ML_WIDTH, d), const), pl.BlockSpec((AT_WIDTH, d), const),
                  pl.BlockSpec((1, d), const), pl.BlockSpec((1, d), const)],
        out_specs=pl.BlockSpec((tm, d), row),
        out_shape=jax.ShapeDtypeStruct((n, d), F32),
        compiler_params=_cparams(1),
        name="out_proj_ln",
    )(ml, at, x2d, w_top, w_bot, g, b)


def _ffn_kernel(x_ref, wg_ref, wu_ref, wd_ref, g_ref, b_ref, o_ref, acc_ref, xb_ref, *, alpha):
    f = pl.program_id(1)

    @pl.when(f == 0)
    def _():
        xb_ref[...] = x_ref[...].astype(BF16)
        acc_ref[...] = jnp.zeros_like(acc_ref)

    xb = xb_ref[...]
    h = _silu(_dot(xb, wg_ref[...])) * _dot(xb, wu_ref[...])
    acc_ref[...] += _dot(h.astype(BF16), wd_ref[...])

    @pl.when(f == pl.num_programs(1) - 1)
    def _():
        o_ref[...] = _layer_norm(alpha * x_ref[...] + acc_ref[...], g_ref[...], b_ref[...])


def _ffn_chunk(d_ff):
    best = LANES
    for c in range(LANES, 1408 + 1, LANES):
        if d_ff % c == 0:
            best = c
    return best


def _dense_ffn_ln(x2d, wg, wu, wd, g, b, alpha):
    n, d = x2d.shape
    d_ff = wg.shape[1]
    tm = 512
    fc = _ffn_chunk(d_ff)
    return pl.pallas_call(
        functools.partial(_ffn_kernel, alpha=alpha),
        grid=(n // tm, d_ff // fc),
        in_specs=[pl.BlockSpec((tm, d), lambda i, f: (i, 0)),
                  pl.BlockSpec((d, fc), lambda i, f: (0, f)),
                  pl.BlockSpec((d, fc), lambda i, f: (0, f)),
                  pl.BlockSpec((fc, d), lambda i, f: (f, 0)),
                  pl.BlockSpec((1, d), lambda i, f: (0, 0)),
                  pl.BlockSpec((1, d), lambda i, f: (0, 0))],
        out_specs=pl.BlockSpec((tm, d), lambda i, f: (i, 0)),
        out_shape=jax.ShapeDtypeStruct((n, d), F32),
        scratch_shapes=[pltpu.VMEM((tm, d), F32), pltpu.VMEM((tm, d), BF16)],
        compiler_params=_cparams(2),
        name="dense_ffn_ln",
    )(x2d, wg, wu, wd, g, b)


def _router_kernel(x_ref, wh_ref, wl_ref, br_ref, comb_ref, dest_ref, cnt_ref):
    tb = x_ref.shape[0]
    x = x_ref[...]
    xh = x.astype(BF16)
    xl = (x - xh.astype(F32)).astype(BF16)
    wh = wh_ref[...]
    lg = (lax.dot_general(wh, xh, NT_DIMS, preferred_element_type=F32)
          + lax.dot_general(wh, xl, NT_DIMS, preferred_element_type=F32)
          + lax.dot_general(wl_ref[...], xh, NT_DIMS, preferred_element_type=F32)) + br_ref[:, 0:1]
    e_id = lax.broadcasted_iota(I32, (N_EXPERTS, tb), 0)
    m1 = jnp.max(lg, axis=0, keepdims=True)
    i1 = jnp.min(jnp.where(lg == m1, e_id, N_EXPERTS), axis=0, keepdims=True)
    lg2 = jnp.where(e_id == i1, -jnp.inf, lg)
    m2 = jnp.max(lg2, axis=0, keepdims=True)
    i2 = jnp.min(jnp.where(lg2 == m2, e_id, N_EXPERTS), axis=0, keepdims=True)
    ex = jnp.exp(m2 - m1)
    g1 = 1.0 / (1.0 + ex)
    g2 = ex / (1.0 + ex)
    comb = jnp.where(e_id == i1, g1, 0.0) + jnp.where(e_id == i2, g2, 0.0)
    comb_ref[...] = comb
    mask = jnp.where((e_id == i1) | (e_id == i2), 1.0, 0.0)
    w = 2 * LANES
    upper = jnp.where(lax.broadcasted_iota(I32, (w, w), 0) < lax.broadcasted_iota(I32, (w, w), 1), 1.0, 0.0).astype(BF16)
    off = jnp.zeros((N_EXPERTS, 1), F32)
    for c in range(tb // w):
        mc = mask[:, c * w:(c + 1) * w]
        dest_ref[:, c * w:(c + 1) * w] = (_dot(mc.astype(BF16), upper) + off).astype(I32)
        off = off + jnp.sum(mc, axis=1, keepdims=True)
    cnt_ref[...] = jnp.broadcast_to(off.astype(I32), cnt_ref.shape)


def _router(x2d, w_hi, w_lo, b_r, tb):
    n, d = x2d.shape
    nb = n // tb
    return pl.pallas_call(
        _router_kernel,
        grid=(nb,),
        in_specs=[pl.BlockSpec((tb, d), lambda i: (i, 0)),
                  pl.BlockSpec((N_EXPERTS, d), lambda i: (0, 0)),
                  pl.BlockSpec((N_EXPERTS, d), lambda i: (0, 0)),
                  pl.BlockSpec((N_EXPERTS, LANES), lambda i: (0, 0))],
        out_specs=[pl.BlockSpec((N_EXPERTS, tb), lambda i: (0, i)),
                   pl.BlockSpec((N_EXPERTS, tb), lambda i: (0, i)),
                   pl.BlockSpec((None, N_EXPERTS, LANES), lambda i: (i, 0, 0))],
        out_shape=[jax.ShapeDtypeStruct((N_EXPERTS, n), F32),
                   jax.ShapeDtypeStruct((N_EXPERTS, n), I32),
                   jax.ShapeDtypeStruct((nb, N_EXPERTS, LANES), I32)],
        compiler_params=_cparams(1),
        name="moe_router",
    )(x2d, w_hi, w_lo, b_r)


def _moe_kernel(cnt_ref, x_ref, dest_ref, comb_ref, wg_ref, wu_ref, wd_ref, g_ref, b_ref, o_ref,
                xb, xe, ye, *, sub, alpha):
    tb = x_ref.shape[0]
    i = pl.program_id(0)
    e = pl.program_id(1)
    f = pl.program_id(2)
    n_e = pl.num_programs(1)
    n_f = pl.num_programs(2)
    nsub = (cnt_ref[i * n_e + e] + sub - 1) // sub

    @pl.when((e == 0) & (f == 0))
    def _():
        xb[...] = x_ref[...].astype(BF16)
        o_ref[...] = jnp.zeros_like(o_ref)

    def rows_of(r):
        return pl.ds(pl.multiple_of(r * sub, sub), sub)

    def onehot(r):
        drow = dest_ref[pl.ds(e, 1), :]
        crow = comb_ref[pl.ds(e, 1), :]
        slot = r * sub + lax.broadcasted_iota(I32, (sub, tb), 0)
        return jnp.where((drow == slot) & (crow > 0.0), 1.0, 0.0).astype(BF16)

    @pl.when(f == 0)
    def _():
        def body(r, carry):
            xe[rows_of(r), :] = _dot(onehot(r), xb[...]).astype(BF16)
            ye[rows_of(r), :] = jnp.zeros((sub, ye.shape[1]), F32)
            return carry
        lax.fori_loop(0, nsub, body, 0)

    def ffn_body(r, carry):
        xr = xe[rows_of(r), :]
        h = _silu(_dot(xr, wg_ref[...])) * _dot(xr, wu_ref[...])
        ye[rows_of(r), :] += _dot(h.astype(BF16), wd_ref[...])
        return carry

    lax.fori_loop(0, nsub, ffn_body, 0)

    @pl.when(f == n_f - 1)
    def _():
        comb = comb_ref[...]
        ch = comb.astype(BF16)
        cl = (comb - ch.astype(F32)).astype(BF16)
        c2 = jnp.concatenate([ch, cl], axis=0)
        lane = lax.broadcasted_iota(I32, (sub, 2 * N_EXPERTS), 1)
        pick = (lane == e) | (lane == e + N_EXPERTS)

        def body(r, carry):
            p = onehot(r)
            gc = lax.dot_general(p, c2, NT_DIMS, preferred_element_type=F32)
            gate = jnp.sum(jnp.where(pick, gc, 0.0), axis=1, keepdims=True)
            yw = (ye[rows_of(r), :] * gate).astype(BF16)
            o_ref[...] += lax.dot_general(p, yw, TN_DIMS, preferred_element_type=F32)
            return carry
        lax.fori_loop(0, nsub, body, 0)

    @pl.when((e == n_e - 1) & (f == n_f - 1))
    def _():
        o_ref[...] = _layer_norm(alpha * x_ref[...] + o_ref[...], g_ref[...], b_ref[...])


def _moe_ffn_ln(x2d, counts, dest_t, comb_t, wg, wu, wd, g, b, alpha, tb):
    n, d = x2d.shape
    n_e, _, d_ff = wg.shape
    fc = 512
    sub = 128
    grid_spec = pltpu.PrefetchScalarGridSpec(
        num_scalar_prefetch=1,
        grid=(n // tb, n_e, d_ff // fc),
        in_specs=[pl.BlockSpec((tb, d), lambda i, e, f, cnt: (i, 0)),
                  pl.BlockSpec((n_e, tb), lambda i, e, f, cnt: (0, i)),
                  pl.BlockSpec((n_e, tb), lambda i, e, f, cnt: (0, i)),
                  pl.BlockSpec((None, d, fc), lambda i, e, f, cnt: (e, 0, f)),
                  pl.BlockSpec((None, d, fc), lambda i, e, f, cnt: (e, 0, f)),
                  pl.BlockSpec((None, fc, d), lambda i, e, f, cnt: (e, f, 0)),
                  pl.BlockSpec((1, d), lambda i, e, f, cnt: (0, 0)),
                  pl.BlockSpec((1, d), lambda i, e, f, cnt: (0, 0))],
        out_specs=pl.BlockSpec((tb, d), lambda i, e, f, cnt: (i, 0)),
        scratch_shapes=[pltpu.VMEM((tb, d), BF16), pltpu.VMEM((tb, d), BF16), pltpu.VMEM((tb, d), F32)],
    )
    return pl.pallas_call(
        functools.partial(_moe_kernel, sub=sub, alpha=alpha),
        grid_spec=grid_spec,
        out_shape=jax.ShapeDtypeStruct((n, d), F32),
        compiler_params=_cparams(3),
        name="moe_ffn_ln",
    )(counts, x2d, dest_t, comb_t, wg, wu, wd, g, b)


def _rope_tables(s):
    half = AT_DIM // 8
    inv = ROPE_THETA ** (-jnp.arange(half, dtype=F32) / half)
    ang = jnp.arange(s).astype(F32)[:, None] * inv[None, :]
    cos = jnp.cos(ang)
    sin = jnp.sin(ang)
    ones = jnp.ones((s, AT_DIM - 2 * half), F32)
    zeros = jnp.zeros((s, AT_DIM - 2 * half), F32)
    z8 = jnp.zeros((s, half), F32)
    cos64 = jnp.concatenate([cos, cos, ones], axis=1)
    sinp64 = jnp.concatenate([z8, sin, zeros], axis=1)
    sinm64 = jnp.concatenate([-sin, z8, zeros], axis=1)
    rep = LANES // AT_DIM
    return (jnp.tile(cos64, (1, rep)), jnp.tile(sinp64, (1, rep)), jnp.tile(sinm64, (1, rep)))


def _permute_w_in(w):
    n_ml = 4 * ML_WIDTH
    n_gate = 2 * ML_HEADS
    pad = PROJ_PAD - w.shape[1]
    return jnp.concatenate([w[:, :n_ml], w[:, n_ml + n_gate:], w[:, n_ml:n_ml + n_gate],
                            jnp.zeros((w.shape[0], pad), w.dtype)], axis=1)


def kernel(x, w_in, ml_conv_w, ml_conv_b, ml_i_b, ml_f_b, ml_norm_g, idx_k_norm_g, idx_k_norm_b, w_out, ln1_g, ln1_b, ln2_g, ln2_b, ffn_w_gate, ffn_w_up, ffn_w_down, moe_w_router, moe_b_router, moe_w_gate, moe_w_up, moe_w_down):
    bsz, s, d = x.shape
    depth = w_in.shape[0]
    alpha = float((2 * depth) ** 0.25)
    n = bsz * s
    cos, sinp, sinm = _rope_tables(s)
    ml_chunk = min(256, s)
    moe_tb = min(1024, n)
    zpad = lambda v, left: jnp.pad(v, (left, LANES - left - v.shape[0]))[None, :]

    x2d = x.reshape(n, d)
    for l in range(depth):
        proj = _in_proj(x2d, _permute_w_in(w_in[l]).astype(BF16)).reshape(bsz, s, PROJ_PAD)
        gate_b = zpad(jnp.concatenate([ml_i_b[l], ml_f_b[l]]), SM_MI)
        ml_out = _mlstm(proj, ml_conv_w[l], ml_conv_b[l][None, :], gate_b, ml_norm_g[l][None, :], ml_chunk)
        q_r, k_r, v_t, iq_r, ik_r, iw_t = _dsa_prep(proj, cos, sinp, sinm,
                                                    zpad(idx_k_norm_g[l], 0), zpad(idx_k_norm_b[l], 0))
        at_out = _dsa_attention(q_r, k_r, v_t, iq_r, ik_r, iw_t)
        wo = w_out[l].astype(BF16)
        x2d = _out_proj_ln(ml_out.reshape(n, ML_WIDTH), at_out.reshape(n, AT_WIDTH), x2d,
                           wo[:ML_WIDTH], wo[ML_WIDTH:], ln1_g[l][None, :], ln1_b[l][None, :], alpha)
        j = l // 2
        if l % 2 == 0:
            x2d = _dense_ffn_ln(x2d, ffn_w_gate[j].astype(BF16), ffn_w_up[j].astype(BF16),
                                ffn_w_down[j].astype(BF16), ln2_g[l][None, :], ln2_b[l][None, :], alpha)
        else:
            wr_t = moe_w_router[j].T
            wr_hi = wr_t.astype(BF16)
            wr_lo = (wr_t - wr_hi.astype(F32)).astype(BF16)
            br = jnp.broadcast_to(moe_b_router[j][:, None], (N_EXPERTS, LANES))
            comb_t, dest_t, cnt = _router(x2d, wr_hi, wr_lo, br, moe_tb)
            counts = cnt[:, :, 0].reshape(-1)
            x2d = _moe_ffn_ln(x2d, counts, dest_t, comb_t, moe_w_gate[j].astype(BF16), moe_w_up[j].astype(BF16),
                              moe_w_down[j].astype(BF16), ln2_g[l][None, :], ln2_b[l][None, :], alpha, moe_tb)
    return x2d.reshape(bsz, s, d)
```

```python
import functools

import jax
import jax.numpy as jnp
import numpy as np
from jax import lax
from jax.experimental import pallas as pl
from jax.experimental.pallas import tpu as pltpu

F32 = jnp.float32
BF16 = jnp.bfloat16
I32 = jnp.int32

ML_HEADS = 4
ML_DIM = 128
ML_WIDTH = ML_HEADS * ML_DIM
CONV_W = 4
AT_HEADS = 8
AT_KV_HEADS = 2
AT_DIM = 64
AT_WIDTH = AT_HEADS * AT_DIM
KV_WIDTH = AT_KV_HEADS * AT_DIM
IDX_HEADS = 4
IDX_DIM = 64
TOPK_MAX = 256
ROPE_THETA = 500000.0
N_EXPERTS = 8
LN_EPS = 1e-5

LANES = 128
SUBLANES = 8
VMEM_LIMIT_BYTES = 56 * 1024 * 1024

COL_MQ = 0
COL_MK = 512
COL_MV = 1024
COL_MO = 1536
COL_AQ = 2048
COL_AKV = 2560
COL_IQ = 2816
COL_SMALL = 3072
PROJ_PAD = 3200
SM_IW = 64
SM_MI = 68
SM_MF = 72

NEG_BIG = -1e30
NT_DIMS = (((1,), (1,)), ((), ()))
TN_DIMS = (((0,), (0,)), ((), ()))


def _cparams(n_axes):
    return pltpu.CompilerParams(dimension_semantics=("arbitrary",) * n_axes,
                                vmem_limit_bytes=VMEM_LIMIT_BYTES)


def _dot(a, b):
    return jnp.dot(a, b, preferred_element_type=F32)


def _layer_norm(z, g, b):
    mu = jnp.mean(z, axis=-1, keepdims=True)
    d = z - mu
    var = jnp.mean(d * d, axis=-1, keepdims=True)
    return d * lax.rsqrt(var + LN_EPS) * g + b


def _silu(x):
    return x / (1.0 + jnp.exp(-x))


def _in_proj_kernel(x_ref, w_ref, o_ref):
    o_ref[...] = _dot(x_ref[...].astype(BF16), w_ref[...])


def _in_proj(x2d, w_bf):
    n, d = x2d.shape
    pw = w_bf.shape[1]
    tm = 512
    return pl.pallas_call(
        _in_proj_kernel,
        grid=(n // tm,),
        in_specs=[pl.BlockSpec((tm, d), lambda i: (i, 0)),
                  pl.BlockSpec((d, pw), lambda i: (0, 0))],
        out_specs=pl.BlockSpec((tm, pw), lambda i: (i, 0)),
        out_shape=jax.ShapeDtypeStruct((n, pw), F32),
        compiler_params=_cparams(1),
        name="in_proj",
    )(x2d, w_bf)


def _mlstm_kernel(q_ref, k_ref, v_ref, o_ref, sm_ref, cw_ref, cb_ref, gb_ref, ng_ref, out_ref,
                  c_st, n_st, m_st, tail):
    chunk = q_ref.shape[0]
    ci = pl.program_id(1)

    @pl.when(ci == 0)
    def _():
        c_st[...] = jnp.zeros_like(c_st)
        n_st[...] = jnp.zeros_like(n_st)
        m_st[...] = jnp.zeros_like(m_st)
        tail[...] = jnp.zeros_like(tail)

    def conv_silu(x, prev, w, b):
        cat = jnp.concatenate([prev, x], axis=0)
        y = pltpu.roll(cat, 3, 0)[SUBLANES:] * w[0:1]
        y = y + pltpu.roll(cat, 2, 0)[SUBLANES:] * w[1:2]
        y = y + pltpu.roll(cat, 1, 0)[SUBLANES:] * w[2:3]
        y = y + x * w[3:4]
        return _silu(y + b)

    xq = q_ref[...]
    xk = k_ref[...]
    cw = cw_ref[...]
    cb = cb_ref[...]
    qs = conv_silu(xq, tail[:, :ML_WIDTH], cw[:, :ML_WIDTH], cb[:, :ML_WIDTH]) * (ML_DIM ** -0.5)
    ks = conv_silu(xk, tail[:, ML_WIDTH:], cw[:, ML_WIDTH:], cb[:, ML_WIDTH:])
    tail[:, :ML_WIDTH] = xq[chunk - SUBLANES:]
    tail[:, ML_WIDTH:] = xk[chunk - SUBLANES:]

    gates = sm_ref[...] + gb_ref[...]
    logf = jnp.minimum(gates, 0.0) - jnp.log1p(jnp.exp(-jnp.abs(gates)))
    row = lax.broadcasted_iota(I32, (chunk, chunk), 0)
    col = lax.broadcasted_iota(I32, (chunk, chunk), 1)
    causal = row >= col
    tri = jnp.where(causal, 1.0, 0.0).astype(BF16)
    hi = logf.astype(BF16)
    r1 = logf - hi.astype(F32)
    mid = r1.astype(BF16)
    lo = (r1 - mid.astype(F32)).astype(BF16)
    bcum = _dot(tri, hi) + _dot(tri, mid) + _dot(tri, lo)
    gates_t = gates.T
    bcum_t = bcum.T

    for h in range(ML_HEADS):
        sl = slice(h * ML_DIM, (h + 1) * ML_DIM)
        qh = qs[:, sl]
        kh = ks[:, sl]
        vb = v_ref[:, sl].astype(BF16)
        b = bcum[:, SM_MF + h:SM_MF + h + 1]
        li = gates[:, SM_MI + h:SM_MI + h + 1]
        r = gates_t[SM_MI + h:SM_MI + h + 1, :] - bcum_t[SM_MF + h:SM_MF + h + 1, :]
        m_prev = m_st[h][0:1, 0:1]
        dmat = jnp.where(causal, b + r, -jnp.inf)
        g_inter = b + m_prev
        m_t = jnp.maximum(g_inter, jnp.max(dmat, axis=1, keepdims=True))
        w_inter = jnp.exp(g_inter - m_t)
        qb = qh.astype(BF16)
        kb = kh.astype(BF16)
        sc = lax.dot_general(qb, kb, NT_DIMS, preferred_element_type=F32) * jnp.exp(dmat - m_t)
        c_old = c_st[h]
        n_old = n_st[h][0:1, :]
        num = w_inter * _dot(qb, c_old.astype(BF16)) + _dot(sc.astype(BF16), vb)
        den = w_inter * jnp.sum(qh * n_old, axis=1, keepdims=True) + jnp.sum(sc, axis=1, keepdims=True)
        hh = num / jnp.maximum(jnp.abs(den), jnp.exp(-m_t))
        b_last = b[chunk - 1:chunk, :]
        g_state = b_last + m_prev
        ls = b_last - b + li
        m_new = jnp.maximum(g_state, jnp.max(ls, axis=0, keepdims=True))
        ws = jnp.exp(ls - m_new)
        decay = jnp.exp(g_state - m_new)
        kw = kh * ws
        c_st[h] = decay * c_old + _dot(kw.T.astype(BF16), vb)
        n_st[h] = jnp.broadcast_to(decay * n_old + jnp.sum(kw, axis=0, keepdims=True), (SUBLANES, ML_DIM))
        m_st[h] = jnp.broadcast_to(m_new, (SUBLANES, LANES))
        mu = jnp.mean(hh, axis=1, keepdims=True)
        d = hh - mu
        var = jnp.mean(d * d, axis=1, keepdims=True)
        hn = d * lax.rsqrt(var + LN_EPS) * ng_ref[:, sl]
        out_ref[:, sl] = (hn / (1.0 + jnp.exp(-o_ref[:, sl]))).astype(out_ref.dtype)


def _mlstm(proj, conv_w, conv_b, gate_b, norm_g, chunk):
    bsz, s, _ = proj.shape
    wblk = ML_WIDTH

    def colspec(col, width):
        return pl.BlockSpec((None, chunk, width), lambda b, c: (b, c, col // width))

    def full2d(shape):
        return pl.BlockSpec(shape, lambda b, c: (0, 0))

    return pl.pallas_call(
        _mlstm_kernel,
        grid=(bsz, s // chunk),
        in_specs=[colspec(COL_MQ, wblk), colspec(COL_MK, wblk), colspec(COL_MV, wblk), colspec(COL_MO, wblk),
                  colspec(COL_SMALL, LANES),
                  full2d((CONV_W, 2 * ML_WIDTH)), full2d((1, 2 * ML_WIDTH)), full2d((1, LANES)),
                  full2d((1, ML_WIDTH))],
        out_specs=pl.BlockSpec((None, chunk, ML_WIDTH), lambda b, c: (b, c, 0)),
        out_shape=jax.ShapeDtypeStruct((bsz, s, ML_WIDTH), BF16),
        scratch_shapes=[pltpu.VMEM((ML_HEADS, ML_DIM, ML_DIM), F32),
                        pltpu.VMEM((ML_HEADS, SUBLANES, ML_DIM), F32),
                        pltpu.VMEM((ML_HEADS, SUBLANES, LANES), F32),
                        pltpu.VMEM((SUBLANES, 2 * ML_WIDTH), F32)],
        compiler_params=_cparams(2),
        name="mlstm",
    )(proj, proj, proj, proj, proj, conv_w, conv_b, gate_b, norm_g)


def _rope(x, cos, sinp, sinm):
    width = x.shape[1]
    reps = width // LANES
    if reps > 1:
        cos = jnp.concatenate([cos] * reps, axis=1)
        sinp = jnp.concatenate([sinp] * reps, axis=1)
        sinm = jnp.concatenate([sinm] * reps, axis=1)
    half = AT_DIM // 8
    return x * cos + pltpu.roll(x, half, 1) * sinp + pltpu.roll(x, width - half, 1) * sinm


def _dsa_prep_kernel(aq_ref, akv_ref, iq_ref, sm_ref, cos_ref, sinp_ref, sinm_ref, lng_ref, lnb_ref,
                     qt_out, k_out, vt_out, iqt_out, ik_out, iwt_out):
    cos = cos_ref[...]
    sinp = sinp_ref[...]
    sinm = sinm_ref[...]
    q = _rope(aq_ref[...], cos, sinp, sinm) * (AT_DIM ** -0.5)
    sm = sm_ref[...]
    lane = lax.broadcasted_iota(I32, sm.shape, 1)
    rep = AT_HEADS // AT_KV_HEADS
    for pair in range(AT_HEADS // 2):
        slab = q[:, pair * LANES:(pair + 1) * LANES]
        swapped = pltpu.roll(slab, AT_DIM, 1)
        for p in range(2):
            h = 2 * pair + p
            g = h // rep
            src = slab if p == g else swapped
            in_group = (lane >= g * AT_DIM) & (lane < (g + 1) * AT_DIM)
            qt_out[h * LANES:(h + 1) * LANES, :] = jnp.where(in_group, src, 0.0).T.astype(BF16)
    akv = akv_ref[...]
    k_out[...] = _rope(akv[:, :KV_WIDTH], cos, sinp, sinm).astype(BF16)
    vt_out[...] = akv[:, KV_WIDTH:].T.astype(BF16)
    iqt_out[...] = (_rope(iq_ref[...], cos, sinp, sinm) * (IDX_DIM ** -0.5)).T.astype(BF16)
    is_k = lane < IDX_DIM
    mu = jnp.sum(jnp.where(is_k, sm, 0.0), axis=1, keepdims=True) * (1.0 / IDX_DIM)
    d = jnp.where(is_k, sm - mu, 0.0)
    var = jnp.sum(d * d, axis=1, keepdims=True) * (1.0 / IDX_DIM)
    y = jnp.where(is_k, d * lax.rsqrt(var + LN_EPS) * lng_ref[...] + lnb_ref[...], 0.0)
    ik_out[...] = _rope(y, cos, sinp, sinm).astype(BF16)
    iwt_out[...] = sm.T[SM_IW:SM_IW + SUBLANES, :] * (IDX_HEADS ** -0.5)


def _dsa_prep(proj, cos, sinp, sinm, ln_g, ln_b):
    bsz, s, _ = proj.shape
    t = 512

    def colspec(col, width):
        return pl.BlockSpec((None, t, width), lambda b, c: (b, c, col // width))

    tab = pl.BlockSpec((t, LANES), lambda b, c: (c, 0))
    vec = pl.BlockSpec((1, LANES), lambda b, c: (0, 0))

    def outspec(width):
        return pl.BlockSpec((None, t, width), lambda b, c: (b, c, 0))

    return pl.pallas_call(
        _dsa_prep_kernel,
        grid=(bsz, s // t),
        in_specs=[colspec(COL_AQ, AT_WIDTH), colspec(COL_AKV, 2 * KV_WIDTH), colspec(COL_IQ, IDX_HEADS * IDX_DIM),
                  colspec(COL_SMALL, LANES), tab, tab, tab, vec, vec],
        out_specs=[pl.BlockSpec((None, AT_HEADS * LANES, t), lambda b, c: (b, 0, c)),
                   outspec(KV_WIDTH),
                   pl.BlockSpec((None, KV_WIDTH, t), lambda b, c: (b, 0, c)),
                   pl.BlockSpec((None, IDX_HEADS * IDX_DIM, t), lambda b, c: (b, 0, c)),
                   outspec(LANES),
                   pl.BlockSpec((None, SUBLANES, t), lambda b, c: (b, 0, c))],
        out_shape=[jax.ShapeDtypeStruct((bsz, AT_HEADS * LANES, s), BF16),
                   jax.ShapeDtypeStruct((bsz, s, KV_WIDTH), BF16),
                   jax.ShapeDtypeStruct((bsz, KV_WIDTH, s), BF16),
                   jax.ShapeDtypeStruct((bsz, IDX_HEADS * IDX_DIM, s), BF16),
                   jax.ShapeDtypeStruct((bsz, s, LANES), BF16),
                   jax.ShapeDtypeStruct((bsz, SUBLANES, s), F32)],
        compiler_params=_cparams(2),
        name="dsa_prep",
    )(proj, proj, proj, proj, cos, sinp, sinm, ln_g, ln_b)


def _dsa_kernel(qt_ref, iqt_ref, iwt_ref, k_ref, vt_ref, ik_ref, out_ref, keys, bias, acc, m_s, l_s, x_s,
                *, topk, seq):
    tq = qt_ref.shape[1]
    ck = tq
    j = pl.program_id(1)
    nck = j + 1
    t_abs = j * tq + lax.broadcasted_iota(I32, (ck, tq), 1)
    s_loc = lax.broadcasted_iota(I32, (ck, tq), 0)
    int_min = jnp.int32(-2 ** 31)
    rep = AT_HEADS // AT_KV_HEADS

    iqt = iqt_ref[...]
    iwt = iwt_ref[...]
    iq_all = jnp.concatenate([iqt[h * IDX_DIM:(h + 1) * IDX_DIM, :] for h in range(IDX_HEADS)], axis=1)
    iw_all = jnp.concatenate([iwt[h:h + 1, :] for h in range(IDX_HEADS)], axis=1)

    def chunk_off(c):
        return pl.multiple_of(c * ck, ck)

    def score_body(c, carry):
        off = chunk_off(c)
        kic = ik_ref[pl.ds(off, ck), :][:, :IDX_DIM]
        sw = jnp.maximum(_dot(kic, iq_all), 0.0) * iw_all
        s = sw[:, :tq]
        for h in range(1, IDX_HEADS):
            s = s + sw[:, h * tq:(h + 1) * tq]
        s = jnp.where(off + s_loc <= t_abs, s, -jnp.inf)
        s = jnp.where(s == 0.0, 0.0, s)
        bits = pltpu.bitcast(s, I32)
        keys[pl.ds(off, ck), :] = bits ^ ((bits >> 31) & jnp.int32(0x7FFFFFFF))
        return carry

    lax.fori_loop(0, nck, score_body, 0)

    def count(pred):
        def body(c, a):
            off = chunk_off(c)
            m = jnp.where(pred(keys[pl.ds(off, ck), :], off), 1, 0)
            return a + jnp.sum(m.reshape(ck // SUBLANES, SUBLANES, tq), axis=0)
        a = lax.fori_loop(0, nck, body, jnp.zeros((SUBLANES, tq), I32))
        return jnp.sum(a, axis=0, keepdims=True)

    c0 = count(lambda kc, off: kc >= 0)
    thr = jnp.where(c0 >= topk, jnp.int32(0), int_min) + jnp.zeros((1, tq), I32)

    def bit_body(i, thr):
        cand = thr + (jnp.int32(1) << (30 - i))
        c = count(lambda kc, off: kc >= cand)
        return jnp.where(c >= topk, cand, thr)

    thr = lax.fori_loop(0, 31, bit_body, thr)

    n_gt = count(lambda kc, off: kc > thr)
    n_eq = count(lambda kc, off: kc == thr)
    need = topk - n_gt
    excess = n_eq > need
    x_s[...] = jnp.full(x_s.shape, seq, I32)

    @pl.when(jnp.max(jnp.where(excess, 1, 0)) > 0)
    def _():
        x = jnp.zeros((1, tq), I32)
        for bit in reversed(range(max(seq - 1, 1).bit_length())):
            cand = x + (1 << bit)
            c = count(lambda kc, off: (kc == thr) & (off + s_loc < cand))
            x = jnp.where(c < need, cand, x)
        x_s[...] = jnp.broadcast_to(jnp.where(excess, x, seq), x_s.shape)

    xlim = x_s[0:1, :]

    def bias_body(c, carry):
        off = chunk_off(c)
        kc = keys[pl.ds(off, ck), :]
        s_abs = off + s_loc
        sel = (kc > thr) | ((kc == thr) & (s_abs <= xlim))
        bias[pl.ds(off, ck), :] = jnp.where(sel & (s_abs <= t_abs), 0.0, NEG_BIG)
        return carry

    lax.fori_loop(0, nck, bias_body, 0)

    m_s[...] = jnp.full(m_s.shape, NEG_BIG, F32)
    l_s[...] = jnp.zeros_like(l_s)
    acc[...] = jnp.zeros_like(acc)

    def att_body(c, carry):
        off = chunk_off(c)
        kc = k_ref[pl.ds(off, ck), :]
        vtc = vt_ref[:, pl.ds(off, ck)]
        bc = bias[pl.ds(off, ck), :]
        b_all = jnp.concatenate([bc] * rep, axis=1)
        for g in range(AT_KV_HEADS):
            qg = jnp.concatenate([qt_ref[(g * rep + r) * LANES:(g * rep + r + 1) * LANES, :] for r in range(rep)],
                                 axis=1)
            lg = _dot(kc, qg) + b_all
            m_old = m_s[g]
            m_new = jnp.maximum(m_old, jnp.max(lg, axis=0, keepdims=True))
            a = jnp.exp(m_old - m_new)
            p = jnp.exp(lg - m_new)
            l_s[g] = a * l_s[g] + jnp.sum(p, axis=0, keepdims=True)
            acc[g] = a * acc[g] + _dot(vtc, p.astype(BF16))
            m_s[g] = m_new
        return carry

    lax.fori_loop(0, nck, att_body, 0)

    lane = lax.broadcasted_iota(I32, (tq, LANES), 1)
    for g in range(AT_KV_HEADS):
        on = acc[g] / l_s[g]
        for pp in range(rep // 2):
            t0 = on[:, 2 * pp * tq:(2 * pp + 1) * tq].T
            t1 = on[:, (2 * pp + 1) * tq:(2 * pp + 2) * tq].T
            if g == 0:
                slab = jnp.where(lane < AT_DIM, t0, pltpu.roll(t1, AT_DIM, 1))
            else:
                slab = jnp.where(lane < AT_DIM, pltpu.roll(t0, AT_DIM, 1), t1)
            pair = g * (rep // 2) + pp
            out_ref[:, pair * LANES:(pair + 1) * LANES] = slab.astype(out_ref.dtype)


def _dsa_attention(q_t, k_r, v_t, iq_t, ik_r, iw_t):
    bsz, s, _ = k_r.shape
    tq = min(256, s)
    topk = min(TOPK_MAX, s // 4)
    rep = AT_HEADS // AT_KV_HEADS
    kern = functools.partial(_dsa_kernel, topk=topk, seq=s)
    return pl.pallas_call(
        kern,
        grid=(bsz, s // tq),
        in_specs=[pl.BlockSpec((None, AT_HEADS * LANES, tq), lambda b, j: (b, 0, j)),
                  pl.BlockSpec((None, IDX_HEADS * IDX_DIM, tq), lambda b, j: (b, 0, j)),
                  pl.BlockSpec((None, SUBLANES, tq), lambda b, j: (b, 0, j)),
                  pl.BlockSpec((None, s, KV_WIDTH), lambda b, j: (b, 0, 0)),
                  pl.BlockSpec((None, KV_WIDTH, s), lambda b, j: (b, 0, 0)),
                  pl.BlockSpec((None, s, LANES), lambda b, j: (b, 0, 0))],
        out_specs=pl.BlockSpec((None, tq, AT_WIDTH), lambda b, j: (b, j, 0)),
        out_shape=jax.ShapeDtypeStruct((bsz, s, AT_WIDTH), BF16),
        scratch_shapes=[pltpu.VMEM((s, tq), I32),
                        pltpu.VMEM((s, tq), F32),
                        pltpu.VMEM((AT_KV_HEADS, LANES, rep * tq), F32),
                        pltpu.VMEM((AT_KV_HEADS, 1, rep * tq), F32),
                        pltpu.VMEM((AT_KV_HEADS, 1, rep * tq), F32),
                        pltpu.VMEM((SUBLANES, tq), I32)],
        compiler_params=_cparams(2),
        name="dsa_attn",
    )(q_t, iq_t, iw_t, k_r, v_t, ik_r)


def _out_ln_kernel(ml_ref, at_ref, x_ref, wt_ref, wb_ref, g_ref, b_ref, o_ref, *, alpha):
    mix = _dot(ml_ref[...], wt_ref[...]) + _dot(at_ref[...], wb_ref[...])
    o_ref[...] = _layer_norm(alpha * x_ref[...] + mix, g_ref[...], b_ref[...])


def _out_proj_ln(ml, at, x2d, w_top, w_bot, g, b, alpha):
    n, d = x2d.shape
    tm = 512
    row = lambda i: (i, 0)
    const = lambda i: (0, 0)
    return pl.pallas_call(
        functools.partial(_out_ln_kernel, alpha=alpha),
        grid=(n // tm,),
        in_specs=[pl.BlockSpec((tm, ML_WIDTH), row), pl.BlockSpec((tm, AT_WIDTH), row), pl.BlockSpec((tm, d), row),
                  pl.BlockSpec((ML_WIDTH, d), const), pl.BlockSpec((AT_WIDTH, d), const),
                  pl.BlockSpec((1, d), const), pl.BlockSpec((1, d), const)],
        out_specs=pl.BlockSpec((tm, d), row),
        out_shape=jax.ShapeDtypeStruct((n, d), F32),
        compiler_params=_cparams(1),
        name="out_proj_ln",
    )(ml, at, x2d, w_top, w_bot, g, b)


def _ffn_kernel(x_ref, wg_ref, wu_ref, wd_ref, g_ref, b_ref, o_ref, acc_ref, xb_ref, *, alpha):
    f = pl.program_id(1)

    @pl.when(f == 0)
    def _():
        xb_ref[...] = x_ref[...].astype(BF16)
        acc_ref[...] = jnp.zeros_like(acc_ref)

    xb = xb_ref[...]
    h = _silu(_dot(xb, wg_ref[...])) * _dot(xb, wu_ref[...])
    acc_ref[...] += _dot(h.astype(BF16), wd_ref[...])

    @pl.when(f == pl.num_programs(1) - 1)
    def _():
        o_ref[...] = _layer_norm(alpha * x_ref[...] + acc_ref[...], g_ref[...], b_ref[...])


def _ffn_chunk(d_ff):
    best = LANES
    for c in range(LANES, 1408 + 1, LANES):
        if d_ff % c == 0:
            best = c
    return best


def _dense_ffn_ln(x2d, wg, wu, wd, g, b, alpha):
    n, d = x2d.shape
    d_ff = wg.shape[1]
    tm = 512
    fc = _ffn_chunk(d_ff)
    return pl.pallas_call(
        functools.partial(_ffn_kernel, alpha=alpha),
        grid=(n // tm, d_ff // fc),
        in_specs=[pl.BlockSpec((tm, d), lambda i, f: (i, 0)),
                  pl.BlockSpec((d, fc), lambda i, f: (0, f)),
                  pl.BlockSpec((d, fc), lambda i, f: (0, f)),
                  pl.BlockSpec((fc, d), lambda i, f: (f, 0)),
                  pl.BlockSpec((1, d), lambda i, f: (0, 0)),
                  pl.BlockSpec((1, d), lambda i, f: (0, 0))],
        out_specs=pl.BlockSpec((tm, d), lambda i, f: (i, 0)),
        out_shape=jax.ShapeDtypeStruct((n, d), F32),
        scratch_shapes=[pltpu.VMEM((tm, d), F32), pltpu.VMEM((tm, d), BF16)],
        compiler_params=_cparams(2),
        name="dense_ffn_ln",
    )(x2d, wg, wu, wd, g, b)


def _router_kernel(x_ref, wh_ref, wl_ref, br_ref, comb_ref, dest_ref, cnt_ref):
    tb = x_ref.shape[0]
    x = x_ref[...]
    xh = x.astype(BF16)
    xl = (x - xh.astype(F32)).astype(BF16)
    wh = wh_ref[...]
    lg = (lax.dot_general(wh, xh, NT_DIMS, preferred_element_type=F32)
          + lax.dot_general(wh, xl, NT_DIMS, preferred_element_type=F32)
          + lax.dot_general(wl_ref[...], xh, NT_DIMS, preferred_element_type=F32)) + br_ref[:, 0:1]
    e_id = lax.broadcasted_iota(I32, (N_EXPERTS, tb), 0)
    m1 = jnp.max(lg, axis=0, keepdims=True)
    i1 = jnp.min(jnp.where(lg == m1, e_id, N_EXPERTS), axis=0, keepdims=True)
    lg2 = jnp.where(e_id == i1, -jnp.inf, lg)
    m2 = jnp.max(lg2, axis=0, keepdims=True)
    i2 = jnp.min(jnp.where(lg2 == m2, e_id, N_EXPERTS), axis=0, keepdims=True)
    ex = jnp.exp(m2 - m1)
    g1 = 1.0 / (1.0 + ex)
    g2 = ex / (1.0 + ex)
    comb = jnp.where(e_id == i1, g1, 0.0) + jnp.where(e_id == i2, g2, 0.0)
    comb_ref[...] = comb
    mask = jnp.where((e_id == i1) | (e_id == i2), 1.0, 0.0)
    w = 2 * LANES
    upper = jnp.where(lax.broadcasted_iota(I32, (w, w), 0) < lax.broadcasted_iota(I32, (w, w), 1), 1.0, 0.0).astype(BF16)
    off = jnp.zeros((N_EXPERTS, 1), F32)
    for c in range(tb // w):
        mc = mask[:, c * w:(c + 1) * w]
        dest_ref[:, c * w:(c + 1) * w] = (_dot(mc.astype(BF16), upper) + off).astype(I32)
        off = off + jnp.sum(mc, axis=1, keepdims=True)
    cnt_ref[...] = jnp.broadcast_to(off.astype(I32), cnt_ref.shape)


def _router(x2d, w_hi, w_lo, b_r, tb):
    n, d = x2d.shape
    nb = n // tb
    return pl.pallas_call(
        _router_kernel,
        grid=(nb,),
        in_specs=[pl.BlockSpec((tb, d), lambda i: (i, 0)),
                  pl.BlockSpec((N_EXPERTS, d), lambda i: (0, 0)),
                  pl.BlockSpec((N_EXPERTS, d), lambda i: (0, 0)),
                  pl.BlockSpec((N_EXPERTS, LANES), lambda i: (0, 0))],
        out_specs=[pl.BlockSpec((N_EXPERTS, tb), lambda i: (0, i)),
                   pl.BlockSpec((N_EXPERTS, tb), lambda i: (0, i)),
                   pl.BlockSpec((None, N_EXPERTS, LANES), lambda i: (i, 0, 0))],
        out_shape=[jax.ShapeDtypeStruct((N_EXPERTS, n), F32),
                   jax.ShapeDtypeStruct((N_EXPERTS, n), I32),
                   jax.ShapeDtypeStruct((nb, N_EXPERTS, LANES), I32)],
        compiler_params=_cparams(1),
        name="moe_router",
    )(x2d, w_hi, w_lo, b_r)


def _moe_kernel(cnt_ref, x_ref, dest_ref, comb_ref, wg_ref, wu_ref, wd_ref, g_ref, b_ref, o_ref,
                xb, xe, ye, *, sub, alpha):
    tb = x_ref.shape[0]
    i = pl.program_id(0)
    e = pl.program_id(1)
    f = pl.program_id(2)
    n_e = pl.num_programs(1)
    n_f = pl.num_programs(2)
    nsub = (cnt_ref[i * n_e + e] + sub - 1) // sub

    @pl.when((e == 0) & (f == 0))
    def _():
        xb[...] = x_ref[...].astype(BF16)
        o_ref[...] = jnp.zeros_like(o_ref)

    def rows_of(r):
        return pl.ds(pl.multiple_of(r * sub, sub), sub)

    def onehot(r):
        drow = dest_ref[pl.ds(e, 1), :]
        crow = comb_ref[pl.ds(e, 1), :]
        slot = r * sub + lax.broadcasted_iota(I32, (sub, tb), 0)
        return jnp.where((drow == slot) & (crow > 0.0), 1.0, 0.0).astype(BF16)

    @pl.when(f == 0)
    def _():
        def body(r, carry):
            xe[rows_of(r), :] = _dot(onehot(r), xb[...]).astype(BF16)
            ye[rows_of(r), :] = jnp.zeros((sub, ye.shape[1]), F32)
            return carry
        lax.fori_loop(0, nsub, body, 0)

    def ffn_body(r, carry):
        xr = xe[rows_of(r), :]
        h = _silu(_dot(xr, wg_ref[...])) * _dot(xr, wu_ref[...])
        ye[rows_of(r), :] += _dot(h.astype(BF16), wd_ref[...])
        return carry

    lax.fori_loop(0, nsub, ffn_body, 0)

    @pl.when(f == n_f - 1)
    def _():
        comb = comb_ref[...]
        ch = comb.astype(BF16)
        cl = (comb - ch.astype(F32)).astype(BF16)
        c2 = jnp.concatenate([ch, cl], axis=0)
        lane = lax.broadcasted_iota(I32, (sub, 2 * N_EXPERTS), 1)
        pick = (lane == e) | (lane == e + N_EXPERTS)

        def body(r, carry):
            p = onehot(r)
            gc = lax.dot_general(p, c2, NT_DIMS, preferred_element_type=F32)
            gate = jnp.sum(jnp.where(pick, gc, 0.0), axis=1, keepdims=True)
            yw = (ye[rows_of(r), :] * gate).astype(BF16)
            o_ref[...] += lax.dot_general(p, yw, TN_DIMS, preferred_element_type=F32)
            return carry
        lax.fori_loop(0, nsub, body, 0)

    @pl.when((e == n_e - 1) & (f == n_f - 1))
    def _():
        o_ref[...] = _layer_norm(alpha * x_ref[...] + o_ref[...], g_ref[...], b_ref[...])


def _moe_ffn_ln(x2d, counts, dest_t, comb_t, wg, wu, wd, g, b, alpha, tb):
    n, d = x2d.shape
    n_e, _, d_ff = wg.shape
    fc = 512
    sub = 128
    grid_spec = pltpu.PrefetchScalarGridSpec(
        num_scalar_prefetch=1,
        grid=(n // tb, n_e, d_ff // fc),
        in_specs=[pl.BlockSpec((tb, d), lambda i, e, f, cnt: (i, 0)),
                  pl.BlockSpec((n_e, tb), lambda i, e, f, cnt: (0, i)),
                  pl.BlockSpec((n_e, tb), lambda i, e, f, cnt: (0, i)),
                  pl.BlockSpec((None, d, fc), lambda i, e, f, cnt: (e, 0, f)),
                  pl.BlockSpec((None, d, fc), lambda i, e, f, cnt: (e, 0, f)),
                  pl.BlockSpec((None, fc, d), lambda i, e, f, cnt: (e, f, 0)),
                  pl.BlockSpec((1, d), lambda i, e, f, cnt: (0, 0)),
                  pl.BlockSpec((1, d), lambda i, e, f, cnt: (0, 0))],
        out_specs=pl.BlockSpec((tb, d), lambda i, e, f, cnt: (i, 0)),
        scratch_shapes=[pltpu.VMEM((tb, d), BF16), pltpu.VMEM((tb, d), BF16), pltpu.VMEM((tb, d), F32)],
    )
    return pl.pallas_call(
        functools.partial(_moe_kernel, sub=sub, alpha=alpha),
        grid_spec=grid_spec,
        out_shape=jax.ShapeDtypeStruct((n, d), F32),
        compiler_params=_cparams(3),
        name="moe_ffn_ln",
    )(counts, x2d, dest_t, comb_t, wg, wu, wd, g, b)


def _rope_tables(s):
    half = AT_DIM // 8
    inv = ROPE_THETA ** (-jnp.arange(half, dtype=F32) / half)
    ang = jnp.arange(s).astype(F32)[:, None] * inv[None, :]
    cos = jnp.cos(ang)
    sin = jnp.sin(ang)
    ones = jnp.ones((s, AT_DIM - 2 * half), F32)
    zeros = jnp.zeros((s, AT_DIM - 2 * half), F32)
    z8 = jnp.zeros((s, half), F32)
    cos64 = jnp.concatenate([cos, cos, ones], axis=1)
    sinp64 = jnp.concatenate([z8, sin, zeros], axis=1)
    sinm64 = jnp.concatenate([-sin, z8, zeros], axis=1)
    rep = LANES // AT_DIM
    return (jnp.tile(cos64, (1, rep)), jnp.tile(sinp64, (1, rep)), jnp.tile(sinm64, (1, rep)))


def _permute_w_in(w):
    n_ml = 4 * ML_WIDTH
    n_gate = 2 * ML_HEADS
    pad = PROJ_PAD - w.shape[1]
    return jnp.concatenate([w[:, :n_ml], w[:, n_ml + n_gate:], w[:, n_ml:n_ml + n_gate],
                            jnp.zeros((w.shape[0], pad), w.dtype)], axis=1)


def kernel(x, w_in, ml_conv_w, ml_conv_b, ml_i_b, ml_f_b, ml_norm_g, idx_k_norm_g, idx_k_norm_b, w_out, ln1_g, ln1_b, ln2_g, ln2_b, ffn_w_gate, ffn_w_up, ffn_w_down, moe_w_router, moe_b_router, moe_w_gate, moe_w_up, moe_w_down):
    bsz, s, d = x.shape
    depth = w_in.shape[0]
    alpha = float((2 * depth) ** 0.25)
    n = bsz * s
    cos, sinp, sinm = _rope_tables(s)
    ml_chunk = min(256, s)
    moe_tb = min(1024, n)
    zpad = lambda v, left: jnp.pad(v, (left, LANES - left - v.shape[0]))[None, :]

    x2d = x.reshape(n, d)
    for l in range(depth):
        proj = _in_proj(x2d, _permute_w_in(w_in[l]).astype(BF16)).reshape(bsz, s, PROJ_PAD)
        gate_b = zpad(jnp.concatenate([ml_i_b[l], ml_f_b[l]]), SM_MI)
        ml_out = _mlstm(proj, ml_conv_w[l], ml_conv_b[l][None, :], gate_b, ml_norm_g[l][None, :], ml_chunk)
        q_r, k_r, v_t, iq_r, ik_r, iw_t = _dsa_prep(proj, cos, sinp, sinm,
                                                    zpad(idx_k_norm_g[l], 0), zpad(idx_k_norm_b[l], 0))
        at_out = _dsa_attention(q_r, k_r, v_t, iq_r, ik_r, iw_t)
        wo = w_out[l].astype(BF16)
        x2d = _out_proj_ln(ml_out.reshape(n, ML_WIDTH), at_out.reshape(n, AT_WIDTH), x2d,
                           wo[:ML_WIDTH], wo[ML_WIDTH:], ln1_g[l][None, :], ln1_b[l][None, :], alpha)
        j = l // 2
        if l % 2 == 0:
            x2d = _dense_ffn_ln(x2d, ffn_w_gate[j].astype(BF16), ffn_w_up[j].astype(BF16),
                                ffn_w_down[j].astype(BF16), ln2_g[l][None, :], ln2_b[l][None, :], alpha)
        else:
            wr_t = moe_w_router[j].T
            wr_hi = wr_t.astype(BF16)
            wr_lo = (wr_t - wr_hi.astype(F32)).astype(BF16)
            br = jnp.broadcast_to(moe_b_router[j][:, None], (N_EXPERTS, LANES))
            comb_t, dest_t, cnt = _router(x2d, wr_hi, wr_lo, br, moe_tb)
            counts = cnt[:, :, 0].reshape(-1)
            x2d = _moe_ffn_ln(x2d, counts, dest_t, comb_t, moe_w_gate[j].astype(BF16), moe_w_up[j].astype(BF16),
                              moe_w_down[j].astype(BF16), ln2_g[l][None, :], ln2_b[l][None, :], alpha, moe_tb)
    return x2d.reshape(bsz, s, d)
```

```python
import functools

import jax
import jax.numpy as jnp
import numpy as np
from jax import lax
from jax.experimental import pallas as pl
from jax.experimental.pallas import tpu as pltpu

F32 = jnp.float32
BF16 = jnp.bfloat16
I32 = jnp.int32

ML_HEADS = 4
ML_DIM = 128
ML_WIDTH = ML_HEADS * ML_DIM
CONV_W = 4
AT_HEADS = 8
AT_KV_HEADS = 2
AT_DIM = 64
AT_WIDTH = AT_HEADS * AT_DIM
KV_WIDTH = AT_KV_HEADS * AT_DIM
IDX_HEADS = 4
IDX_DIM = 64
TOPK_MAX = 256
ROPE_THETA = 500000.0
N_EXPERTS = 8
LN_EPS = 1e-5

LANES = 128
SUBLANES = 8
VMEM_LIMIT_BYTES = 56 * 1024 * 1024

COL_MQ = 0
COL_MK = 512
COL_MV = 1024
COL_MO = 1536
COL_AQ = 2048
COL_AKV = 2560
COL_IQ = 2816
COL_SMALL = 3072
PROJ_PAD = 3200
SM_IW = 64
SM_MI = 68
SM_MF = 72

NEG_BIG = -1e30
NT_DIMS = (((1,), (1,)), ((), ()))
TN_DIMS = (((0,), (0,)), ((), ()))


def _cparams(n_axes):
    return pltpu.CompilerParams(dimension_semantics=("arbitrary",) * n_axes,
                                vmem_limit_bytes=VMEM_LIMIT_BYTES)


def _dot(a, b):
    return jnp.dot(a, b, preferred_element_type=F32)


def _layer_norm(z, g, b):
    mu = jnp.mean(z, axis=-1, keepdims=True)
    d = z - mu
    var = jnp.mean(d * d, axis=-1, keepdims=True)
    return d * lax.rsqrt(var + LN_EPS) * g + b


def _silu(x):
    return x / (1.0 + jnp.exp(-x))


def _in_proj_kernel(x_ref, w_ref, o_ref):
    o_ref[...] = _dot(x_ref[...].astype(BF16), w_ref[...])


def _in_proj(x2d, w_bf):
    n, d = x2d.shape
    pw = w_bf.shape[1]
    tm = 512
    return pl.pallas_call(
        _in_proj_kernel,
        grid=(n // tm,),
        in_specs=[pl.BlockSpec((tm, d), lambda i: (i, 0)),
                  pl.BlockSpec((d, pw), lambda i: (0, 0))],
        out_specs=pl.BlockSpec((tm, pw), lambda i: (i, 0)),
        out_shape=jax.ShapeDtypeStruct((n, pw), F32),
        compiler_params=_cparams(1),
        name="in_proj",
    )(x2d, w_bf)


def _mlstm_kernel(q_ref, k_ref, v_ref, o_ref, sm_ref, cw_ref, cb_ref, gb_ref, ng_ref, out_ref,
                  c_st, n_st, m_st, tail):
    chunk = q_ref.shape[0]
    ci = pl.program_id(1)

    @pl.when(ci == 0)
    def _():
        c_st[...] = jnp.zeros_like(c_st)
        n_st[...] = jnp.zeros_like(n_st)
        m_st[...] = jnp.zeros_like(m_st)
        tail[...] = jnp.zeros_like(tail)

    def conv_silu(x, prev, w, b):
        cat = jnp.concatenate([prev, x], axis=0)
        y = pltpu.roll(cat, 3, 0)[SUBLANES:] * w[0:1]
        y = y + pltpu.roll(cat, 2, 0)[SUBLANES:] * w[1:2]
        y = y + pltpu.roll(cat, 1, 0)[SUBLANES:] * w[2:3]
        y = y + x * w[3:4]
        return _silu(y + b)

    xq = q_ref[...]
    xk = k_ref[...]
    cw = cw_ref[...]
    cb = cb_ref[...]
    qs = conv_silu(xq, tail[:, :ML_WIDTH], cw[:, :ML_WIDTH], cb[:, :ML_WIDTH]) * (ML_DIM ** -0.5)
    ks = conv_silu(xk, tail[:, ML_WIDTH:], cw[:, ML_WIDTH:], cb[:, ML_WIDTH:])
    tail[:, :ML_WIDTH] = xq[chunk - SUBLANES:]
    tail[:, ML_WIDTH:] = xk[chunk - SUBLANES:]

    gates = sm_ref[...] + gb_ref[...]
    logf = jnp.minimum(gates, 0.0) - jnp.log1p(jnp.exp(-jnp.abs(gates)))
    row = lax.broadcasted_iota(I32, (chunk, chunk), 0)
    col = lax.broadcasted_iota(I32, (chunk, chunk), 1)
    causal = row >= col
    tri = jnp.where(causal, 1.0, 0.0).astype(BF16)
    hi = logf.astype(BF16)
    r1 = logf - hi.astype(F32)
    mid = r1.astype(BF16)
    lo = (r1 - mid.astype(F32)).astype(BF16)
    bcum = _dot(tri, hi) + _dot(tri, mid) + _dot(tri, lo)
    gates_t = gates.T
    bcum_t = bcum.T

    for h in range(ML_HEADS):
        sl = slice(h * ML_DIM, (h + 1) * ML_DIM)
        qh = qs[:, sl]
        kh = ks[:, sl]
        vb = v_ref[:, sl].astype(BF16)
        b = bcum[:, SM_MF + h:SM_MF + h + 1]
        li = gates[:, SM_MI + h:SM_MI + h + 1]
        r = gates_t[SM_MI + h:SM_MI + h + 1, :] - bcum_t[SM_MF + h:SM_MF + h + 1, :]
        m_prev = m_st[h][0:1, 0:1]
        dmat = jnp.where(causal, b + r, -jnp.inf)
        g_inter = b + m_prev
        m_t = jnp.maximum(g_inter, jnp.max(dmat, axis=1, keepdims=True))
        w_inter = jnp.exp(g_inter - m_t)
        qb = qh.astype(BF16)
        kb = kh.astype(BF16)
        sc = lax.dot_general(qb, kb, NT_DIMS, preferred_element_type=F32) * jnp.exp(dmat - m_t)
        c_old = c_st[h]
        n_old = n_st[h][0:1, :]
        num = w_inter * _dot(qb, c_old.astype(BF16)) + _dot(sc.astype(BF16), vb)
        den = w_inter * jnp.sum(qh * n_old, axis=1, keepdims=True) + jnp.sum(sc, axis=1, keepdims=True)
        hh = num / jnp.maximum(jnp.abs(den), jnp.exp(-m_t))
        b_last = b[chunk - 1:chunk, :]
        g_state = b_last + m_prev
        ls = b_last - b + li
        m_new = jnp.maximum(g_state, jnp.max(ls, axis=0, keepdims=True))
        ws = jnp.exp(ls - m_new)
        decay = jnp.exp(g_state - m_new)
        kw = kh * ws
        c_st[h] = decay * c_old + _dot(kw.T.astype(BF16), vb)
        n_st[h] = jnp.broadcast_to(decay * n_old + jnp.sum(kw, axis=0, keepdims=True), (SUBLANES, ML_DIM))
        m_st[h] = jnp.broadcast_to(m_new, (SUBLANES, LANES))
        mu = jnp.mean(hh, axis=1, keepdims=True)
        d = hh - mu
        var = jnp.mean(d * d, axis=1, keepdims=True)
        hn = d * lax.rsqrt(var + LN_EPS) * ng_ref[:, sl]
        out_ref[:, sl] = (hn / (1.0 + jnp.exp(-o_ref[:, sl]))).astype(out_ref.dtype)


def _mlstm(proj, conv_w, conv_b, gate_b, norm_g, chunk):
    bsz, s, _ = proj.shape
    wblk = ML_WIDTH

    def colspec(col, width):
        return pl.BlockSpec((None, chunk, width), lambda b, c: (b, c, col // width))

    def full2d(shape):
        return pl.BlockSpec(shape, lambda b, c: (0, 0))

    return pl.pallas_call(
        _mlstm_kernel,
        grid=(bsz, s // chunk),
        in_specs=[colspec(COL_MQ, wblk), colspec(COL_MK, wblk), colspec(COL_MV, wblk), colspec(COL_MO, wblk),
                  colspec(COL_SMALL, LANES),
                  full2d((CONV_W, 2 * ML_WIDTH)), full2d((1, 2 * ML_WIDTH)), full2d((1, LANES)),
                  full2d((1, ML_WIDTH))],
        out_specs=pl.BlockSpec((None, chunk, ML_WIDTH), lambda b, c: (b, c, 0)),
        out_shape=jax.ShapeDtypeStruct((bsz, s, ML_WIDTH), BF16),
        scratch_shapes=[pltpu.VMEM((ML_HEADS, ML_DIM, ML_DIM), F32),
                        pltpu.VMEM((ML_HEADS, SUBLANES, ML_DIM), F32),
                        pltpu.VMEM((ML_HEADS, SUBLANES, LANES), F32),
                        pltpu.VMEM((SUBLANES, 2 * ML_WIDTH), F32)],
        compiler_params=_cparams(2),
        name="mlstm",
    )(proj, proj, proj, proj, proj, conv_w, conv_b, gate_b, norm_g)


def _rope(x, cos, sinp, sinm):
    width = x.shape[1]
    reps = width // LANES
    if reps > 1:
        cos = jnp.concatenate([cos] * reps, axis=1)
        sinp = jnp.concatenate([sinp] * reps, axis=1)
        sinm = jnp.concatenate([sinm] * reps, axis=1)
    half = AT_DIM // 8
    return x * cos + pltpu.roll(x, half, 1) * sinp + pltpu.roll(x, width - half, 1) * sinm


def _dsa_prep_kernel(aq_ref, akv_ref, iq_ref, sm_ref, cos_ref, sinp_ref, sinm_ref, lng_ref, lnb_ref,
                     qt_out, k_out, vt_out, iqt_out, ik_out, iwt_out):
    cos = cos_ref[...]
    sinp = sinp_ref[...]
    sinm = sinm_ref[...]
    q = _rope(aq_ref[...], cos, sinp, sinm) * (AT_DIM ** -0.5)
    sm = sm_ref[...]
    lane = lax.broadcasted_iota(I32, sm.shape, 1)
    rep = AT_HEADS // AT_KV_HEADS
    for pair in range(AT_HEADS // 2):
        slab = q[:, pair * LANES:(pair + 1) * LANES]
        swapped = pltpu.roll(slab, AT_DIM, 1)
        for p in range(2):
            h = 2 * pair + p
            g = h // rep
            src = slab if p == g else swapped
            in_group = (lane >= g * AT_DIM) & (lane < (g + 1) * AT_DIM)
            qt_out[h * LANES:(h + 1) * LANES, :] = jnp.where(in_group, src, 0.0).T.astype(BF16)
    akv = akv_ref[...]
    k_out[...] = _rope(akv[:, :KV_WIDTH], cos, sinp, sinm).astype(BF16)
    vt_out[...] = akv[:, KV_WIDTH:].T.astype(BF16)
    iqt_out[...] = (_rope(iq_ref[...], cos, sinp, sinm) * (IDX_DIM ** -0.5)).T.astype(BF16)
    is_k = lane < IDX_DIM
    mu = jnp.sum(jnp.where(is_k, sm, 0.0), axis=1, keepdims=True) * (1.0 / IDX_DIM)
    d = jnp.where(is_k, sm - mu, 0.0)
    var = jnp.sum(d * d, axis=1, keepdims=True) * (1.0 / IDX_DIM)
    y = jnp.where(is_k, d * lax.rsqrt(var + LN_EPS) * lng_ref[...] + lnb_ref[...], 0.0)
    ik_out[...] = _rope(y, cos, sinp, sinm).astype(BF16)
    iwt_out[...] = sm.T[SM_IW:SM_IW + SUBLANES, :] * (IDX_HEADS ** -0.5)


def _dsa_prep(proj, cos, sinp, sinm, ln_g, ln_b):
    bsz, s, _ = proj.shape
    t = 512

    def colspec(col, width):
        return pl.BlockSpec((None, t, width), lambda b, c: (b, c, col // width))

    tab = pl.BlockSpec((t, LANES), lambda b, c: (c, 0))
    vec = pl.BlockSpec((1, LANES), lambda b, c: (0, 0))

    def outspec(width):
        return pl.BlockSpec((None, t, width), lambda b, c: (b, c, 0))

    return pl.pallas_call(
        _dsa_prep_kernel,
        grid=(bsz, s // t),
        in_specs=[colspec(COL_AQ, AT_WIDTH), colspec(COL_AKV, 2 * KV_WIDTH), colspec(COL_IQ, IDX_HEADS * IDX_DIM),
                  colspec(COL_SMALL, LANES), tab, tab, tab, vec, vec],
        out_specs=[pl.BlockSpec((None, AT_HEADS * LANES, t), lambda b, c: (b, 0, c)),
                   outspec(KV_WIDTH),
                   pl.BlockSpec((None, KV_WIDTH, t), lambda b, c: (b, 0, c)),
                   pl.BlockSpec((None, IDX_HEADS * IDX_DIM, t), lambda b, c: (b, 0, c)),
                   outspec(LANES),
                   pl.BlockSpec((None, SUBLANES, t), lambda b, c: (b, 0, c))],
        out_shape=[jax.ShapeDtypeStruct((bsz, AT_HEADS * LANES, s), BF16),
                   jax.ShapeDtypeStruct((bsz, s, KV_WIDTH), BF16),
                   jax.ShapeDtypeStruct((bsz, KV_WIDTH, s), BF16),
                   jax.ShapeDtypeStruct((bsz, IDX_HEADS * IDX_DIM, s), BF16),
                   jax.ShapeDtypeStruct((bsz, s, LANES), BF16),
                   jax.ShapeDtypeStruct((bsz, SUBLANES, s), F32)],
        compiler_params=_cparams(2),
        name="dsa_prep",
    )(proj, proj, proj, proj, cos, sinp, sinm, ln_g, ln_b)


def _tree_sum(parts):
    while len(parts) > 1:
        parts = [parts[i] + parts[i + 1] for i in range(0, len(parts) - 1, 2)] + ([parts[-1]] if len(parts) % 2 else [])
    return parts[0]


def _dsa_kernel(qt_ref, iqt_ref, iwt_ref, k_ref, vt_ref, ik_ref, out_ref, keys, digit, bias, acc, m_s, l_s, x_s,
                *, topk, seq, ck):
    tq = qt_ref.shape[1]
    j = pl.program_id(1)
    nck = (j * tq + tq + ck - 1) // ck
    t_abs = j * tq + lax.broadcasted_iota(I32, (ck, tq), 1)
    s_loc = lax.broadcasted_iota(I32, (ck, tq), 0)
    rep = AT_HEADS // AT_KV_HEADS
    pack = 2 * SUBLANES

    iqt = iqt_ref[...]
    iwt = iwt_ref[...]
    iq_all = jnp.concatenate([iqt[h * IDX_DIM:(h + 1) * IDX_DIM, :] for h in range(IDX_HEADS)], axis=1)
    iw_all = jnp.concatenate([iwt[h:h + 1, :] for h in range(IDX_HEADS)], axis=1)

    def chunk_off(c):
        return pl.multiple_of(c * ck, ck)

    def score_body(c, carry):
        off = chunk_off(c)
        kic = ik_ref[pl.ds(off, ck), :][:, :IDX_DIM]
        sw = jnp.maximum(_dot(kic, iq_all), 0.0) * iw_all
        s = sw[:, :tq]
        for h in range(1, IDX_HEADS):
            s = s + sw[:, h * tq:(h + 1) * tq]
        s = jnp.where(off + s_loc <= t_abs, s, -jnp.inf)
        s = jnp.where(s == 0.0, 0.0, s)
        bits = pltpu.bitcast(s, I32)
        key = bits ^ ((bits >> 31) & jnp.int32(0x7FFFFFFF))
        keys[pl.ds(off, ck), :] = key
        digit[pl.ds(off, ck), :] = ((key >> 24) + 128).astype(F32).astype(BF16)
        return carry

    lax.fori_loop(0, nck, score_body, 0)

    def count(pred):
        def body(c, a):
            off = chunk_off(c)
            m = jnp.where(pred(keys[pl.ds(off, ck), :], off), 1, 0)
            return a + jnp.sum(m.reshape(ck // SUBLANES, SUBLANES, tq), axis=0)
        a = lax.fori_loop(0, nck, body, jnp.zeros((SUBLANES, tq), I32))
        return jnp.sum(a, axis=0, keepdims=True)

    def count_digit(pred):
        one = jnp.ones((), BF16)
        zero = jnp.zeros((), BF16)

        def body(c, a):
            m = jnp.where(pred(digit[pl.ds(chunk_off(c), ck), :]), one, zero).reshape(ck // pack, pack, tq)
            return a + _tree_sum([m[i] for i in range(ck // pack)]).astype(F32)
        a = lax.fori_loop(0, nck, body, jnp.zeros((pack, tq), F32))
        return jnp.sum(a, axis=0, keepdims=True).astype(I32)

    k_rem = jnp.full((1, tq), topk, I32)
    n_gt = jnp.zeros((1, tq), I32)
    prefix = jnp.zeros((1, tq), I32)
    for byte in range(4):
        if byte > 0:
            shift = 32 - 8 * byte
            want = prefix

            def build_body(c, carry, shift=shift, want=want):
                rows = pl.ds(chunk_off(c), ck)
                kc = keys[rows, :]
                d = jnp.where((kc >> shift) == want, (kc >> (shift - 8)) & 0xFF, -1)
                digit[rows, :] = d.astype(F32).astype(BF16)
                return carry

            lax.fori_loop(0, nck, build_body, 0)
        t = jnp.zeros((1, tq), F32)
        for bit in reversed(range(8)):
            cand = t + float(1 << bit)
            cand_b = cand.astype(BF16)
            c = count_digit(lambda dc, cand_b=cand_b: dc >= cand_b)
            t = jnp.where(c >= k_rem, cand, t)
        t_b = t.astype(BF16)
        c_gt = count_digit(lambda dc: dc > t_b)
        n_gt = n_gt + c_gt
        k_rem = k_rem - c_gt
        t_i = t.astype(I32)
        prefix = (t_i - 128) if byte == 0 else ((prefix << 8) | t_i)
    thr = prefix
    n_eq = count_digit(lambda dc: dc == t_b)

    need = topk - n_gt
    excess = n_eq > need
    x_s[...] = jnp.full(x_s.shape, seq, I32)

    @pl.when(jnp.max(jnp.where(excess, 1, 0)) > 0)
    def _():
        x = jnp.zeros((1, tq), I32)
        for bit in reversed(range(max(seq - 1, 1).bit_length())):
            cand = x + (1 << bit)
            c = count(lambda kc, off: (kc == thr) & (off + s_loc < cand))
            x = jnp.where(c < need, cand, x)
        x_s[...] = jnp.broadcast_to(jnp.where(excess, x, seq), x_s.shape)

    xlim = x_s[0:1, :]

    def bias_body(c, carry):
        off = chunk_off(c)
        kc = keys[pl.ds(off, ck), :]
        s_abs = off + s_loc
        sel = (kc > thr) | ((kc == thr) & (s_abs <= xlim))
        bias[pl.ds(off, ck), :] = jnp.where(sel & (s_abs <= t_abs), 0.0, NEG_BIG)
        return carry

    lax.fori_loop(0, nck, bias_body, 0)

    m_s[...] = jnp.full(m_s.shape, NEG_BIG, F32)
    l_s[...] = jnp.zeros_like(l_s)
    acc[...] = jnp.zeros_like(acc)

    def att_body(c, carry):
        off = chunk_off(c)
        kc = k_ref[pl.ds(off, ck), :]
        vtc = vt_ref[:, pl.ds(off, ck)]
        bc = bias[pl.ds(off, ck), :]
        b_all = jnp.concatenate([bc] * rep, axis=1)
        for g in range(AT_KV_HEADS):
            qg = jnp.concatenate([qt_ref[(g * rep + r) * LANES:(g * rep + r + 1) * LANES, :] for r in range(rep)],
                                 axis=1)
            lg = _dot(kc, qg) + b_all
            m_old = m_s[g]
            m_new = jnp.maximum(m_old, jnp.max(lg, axis=0, keepdims=True))
            a = jnp.exp(m_old - m_new)
            p = jnp.exp(lg - m_new)
            l_s[g] = a * l_s[g] + jnp.sum(p, axis=0, keepdims=True)
            acc[g] = a * acc[g] + _dot(vtc, p.astype(BF16))
            m_s[g] = m_new
        return carry

    lax.fori_loop(0, nck, att_body, 0)

    lane = lax.broadcasted_iota(I32, (tq, LANES), 1)
    for g in range(AT_KV_HEADS):
        on = acc[g] / l_s[g]
        for pp in range(rep // 2):
            t0 = on[:, 2 * pp * tq:(2 * pp + 1) * tq].T
            t1 = on[:, (2 * pp + 1) * tq:(2 * pp + 2) * tq].T
            if g == 0:
                slab = jnp.where(lane < AT_DIM, t0, pltpu.roll(t1, AT_DIM, 1))
            else:
                slab = jnp.where(lane < AT_DIM, pltpu.roll(t0, AT_DIM, 1), t1)
            pair = g * (rep // 2) + pp
            out_ref[:, pair * LANES:(pair + 1) * LANES] = slab.astype(out_ref.dtype)


def _dsa_attention(q_t, k_r, v_t, iq_t, ik_r, iw_t):
    bsz, s, _ = k_r.shape
    tq = min(256, s)
    topk = min(TOPK_MAX, s // 4)
    rep = AT_HEADS // AT_KV_HEADS
    ck = min(512, s)
    assert s % ck == 0 and ck >= topk and ck % tq == 0
    kern = functools.partial(_dsa_kernel, topk=topk, seq=s, ck=ck)
    return pl.pallas_call(
        kern,
        grid=(bsz, s // tq),
        in_specs=[pl.BlockSpec((None, AT_HEADS * LANES, tq), lambda b, j: (b, 0, j)),
                  pl.BlockSpec((None, IDX_HEADS * IDX_DIM, tq), lambda b, j: (b, 0, j)),
                  pl.BlockSpec((None, SUBLANES, tq), lambda b, j: (b, 0, j)),
                  pl.BlockSpec((None, s, KV_WIDTH), lambda b, j: (b, 0, 0)),
                  pl.BlockSpec((None, KV_WIDTH, s), lambda b, j: (b, 0, 0)),
                  pl.BlockSpec((None, s, LANES), lambda b, j: (b, 0, 0))],
        out_specs=pl.BlockSpec((None, tq, AT_WIDTH), lambda b, j: (b, j, 0)),
        out_shape=jax.ShapeDtypeStruct((bsz, s, AT_WIDTH), BF16),
        scratch_shapes=[pltpu.VMEM((s, tq), I32),
                        pltpu.VMEM((s, tq), BF16),
                        pltpu.VMEM((s, tq), F32),
                        pltpu.VMEM((AT_KV_HEADS, LANES, rep * tq), F32),
                        pltpu.VMEM((AT_KV_HEADS, 1, rep * tq), F32),
                        pltpu.VMEM((AT_KV_HEADS, 1, rep * tq), F32),
                        pltpu.VMEM((SUBLANES, tq), I32)],
        compiler_params=_cparams(2),
        name="dsa_attn",
    )(q_t, iq_t, iw_t, k_r, v_t, ik_r)


def _out_ln_kernel(ml_ref, at_ref, x_ref, wt_ref, wb_ref, g_ref, b_ref, o_ref, *, alpha):
    mix = _dot(ml_ref[...], wt_ref[...]) + _dot(at_ref[...], wb_ref[...])
    o_ref[...] = _layer_norm(alpha * x_ref[...] + mix, g_ref[...], b_ref[...])


def _out_proj_ln(ml, at, x2d, w_top, w_bot, g, b, alpha):
    n, d = x2d.shape
    tm = 512
    row = lambda i: (i, 0)
    const = lambda i: (0, 0)
    return pl.pallas_call(
        functools.partial(_out_ln_kernel, alpha=alpha),
        grid=(n // tm,),
        in_specs=[pl.BlockSpec((tm, ML_WIDTH), row), pl.BlockSpec((tm, AT_WIDTH), row), pl.BlockSpec((tm, d), row),
                  pl.BlockSpec((ML_WIDTH, d), const), pl.BlockSpec((AT_WIDTH, d), const),
                  pl.BlockSpec((1, d), const), pl.BlockSpec((1, d), const)],
        out_specs=pl.BlockSpec((tm, d), row),
        out_shape=jax.ShapeDtypeStruct((n, d), F32),
        compiler_params=_cparams(1),
        name="out_proj_ln",
    )(ml, at, x2d, w_top, w_bot, g, b)


def _ffn_kernel(x_ref, wg_ref, wu_ref, wd_ref, g_ref, b_ref, o_ref, acc_ref, xb_ref, *, alpha):
    f = pl.program_id(1)

    @pl.when(f == 0)
    def _():
        xb_ref[...] = x_ref[...].astype(BF16)
        acc_ref[...] = jnp.zeros_like(acc_ref)

    xb = xb_ref[...]
    h = _silu(_dot(xb, wg_ref[...])) * _dot(xb, wu_ref[...])
    acc_ref[...] += _dot(h.astype(BF16), wd_ref[...])

    @pl.when(f == pl.num_programs(1) - 1)
    def _():
        o_ref[...] = _layer_norm(alpha * x_ref[...] + acc_ref[...], g_ref[...], b_ref[...])


def _ffn_chunk(d_ff):
    best = LANES
    for c in range(LANES, 1408 + 1, LANES):
        if d_ff % c == 0:
            best = c
    return best


def _dense_ffn_ln(x2d, wg, wu, wd, g, b, alpha):
    n, d = x2d.shape
    d_ff = wg.shape[1]
    tm = 512
    fc = _ffn_chunk(d_ff)
    return pl.pallas_call(
        functools.partial(_ffn_kernel, alpha=alpha),
        grid=(n // tm, d_ff // fc),
        in_specs=[pl.BlockSpec((tm, d), lambda i, f: (i, 0)),
                  pl.BlockSpec((d, fc), lambda i, f: (0, f)),
                  pl.BlockSpec((d, fc), lambda i, f: (0, f)),
                  pl.BlockSpec((fc, d), lambda i, f: (f, 0)),
                  pl.BlockSpec((1, d), lambda i, f: (0, 0)),
                  pl.BlockSpec((1, d), lambda i, f: (0, 0))],
        out_specs=pl.BlockSpec((tm, d), lambda i, f: (i, 0)),
        out_shape=jax.ShapeDtypeStruct((n, d), F32),
        scratch_shapes=[pltpu.VMEM((tm, d), F32), pltpu.VMEM((tm, d), BF16)],
        compiler_params=_cparams(2),
        name="dense_ffn_ln",
    )(x2d, wg, wu, wd, g, b)


def _router_kernel(x_ref, wh_ref, wl_ref, br_ref, comb_ref, dest_ref, cnt_ref):
    tb = x_ref.shape[0]
    x = x_ref[...]
    xh = x.astype(BF16)
    xl = (x - xh.astype(F32)).astype(BF16)
    wh = wh_ref[...]
    lg = (lax.dot_general(wh, xh, NT_DIMS, preferred_element_type=F32)
          + lax.dot_general(wh, xl, NT_DIMS, preferred_element_type=F32)
          + lax.dot_general(wl_ref[...], xh, NT_DIMS, preferred_element_type=F32)) + br_ref[:, 0:1]
    e_id = lax.broadcasted_iota(I32, (N_EXPERTS, tb), 0)
    m1 = jnp.max(lg, axis=0, keepdims=True)
    i1 = jnp.min(jnp.where(lg == m1, e_id, N_EXPERTS), axis=0, keepdims=True)
    lg2 = jnp.where(e_id == i1, -jnp.inf, lg)
    m2 = jnp.max(lg2, axis=0, keepdims=True)
    i2 = jnp.min(jnp.where(lg2 == m2, e_id, N_EXPERTS), axis=0, keepdims=True)
    ex = jnp.exp(m2 - m1)
    g1 = 1.0 / (1.0 + ex)
    g2 = ex / (1.0 + ex)
    comb = jnp.where(e_id == i1, g1, 0.0) + jnp.where(e_id == i2, g2, 0.0)
    comb_ref[...] = comb
    mask = jnp.where((e_id == i1) | (e_id == i2), 1.0, 0.0)
    w = 2 * LANES
    upper = jnp.where(lax.broadcasted_iota(I32, (w, w), 0) < lax.broadcasted_iota(I32, (w, w), 1), 1.0, 0.0).astype(BF16)
    off = jnp.zeros((N_EXPERTS, 1), F32)
    for c in range(tb // w):
        mc = mask[:, c * w:(c + 1) * w]
        dest_ref[:, c * w:(c + 1) * w] = (_dot(mc.astype(BF16), upper) + off).astype(I32)
        off = off + jnp.sum(mc, axis=1, keepdims=True)
    cnt_ref[...] = jnp.broadcast_to(off.astype(I32), cnt_ref.shape)


def _router(x2d, w_hi, w_lo, b_r, tb):
    n, d = x2d.shape
    nb = n // tb
    return pl.pallas_call(
        _router_kernel,
        grid=(nb,),
        in_specs=[pl.BlockSpec((tb, d), lambda i: (i, 0)),
                  pl.BlockSpec((N_EXPERTS, d), lambda i: (0, 0)),
                  pl.BlockSpec((N_EXPERTS, d), lambda i: (0, 0)),
                  pl.BlockSpec((N_EXPERTS, LANES), lambda i: (0, 0))],
        out_specs=[pl.BlockSpec((N_EXPERTS, tb), lambda i: (0, i)),
                   pl.BlockSpec((N_EXPERTS, tb), lambda i: (0, i)),
                   pl.BlockSpec((None, N_EXPERTS, LANES), lambda i: (i, 0, 0))],
        out_shape=[jax.ShapeDtypeStruct((N_EXPERTS, n), F32),
                   jax.ShapeDtypeStruct((N_EXPERTS, n), I32),
                   jax.ShapeDtypeStruct((nb, N_EXPERTS, LANES), I32)],
        compiler_params=_cparams(1),
        name="moe_router",
    )(x2d, w_hi, w_lo, b_r)


def _moe_kernel(cnt_ref, x_ref, dest_ref, comb_ref, wg_ref, wu_ref, wd_ref, g_ref, b_ref, o_ref,
                xb, xe, ye, *, sub, alpha):
    tb = x_ref.shape[0]
    i = pl.program_id(0)
    e = pl.program_id(1)
    f = pl.program_id(2)
    n_e = pl.num_programs(1)
    n_f = pl.num_programs(2)
    nsub = (cnt_ref[i * n_e + e] + sub - 1) // sub

    @pl.when((e == 0) & (f == 0))
    def _():
        xb[...] = x_ref[...].astype(BF16)
        o_ref[...] = jnp.zeros_like(o_ref)

    def rows_of(r):
        return pl.ds(pl.multiple_of(r * sub, sub), sub)

    def onehot(r):
        drow = dest_ref[pl.ds(e, 1), :]
        crow = comb_ref[pl.ds(e, 1), :]
        slot = r * sub + lax.broadcasted_iota(I32, (sub, tb), 0)
        return jnp.where((drow == slot) & (crow > 0.0), 1.0, 0.0).astype(BF16)

    @pl.when(f == 0)
    def _():
        def body(r, carry):
            xe[rows_of(r), :] = _dot(onehot(r), xb[...]).astype(BF16)
            ye[rows_of(r), :] = jnp.zeros((sub, ye.shape[1]), F32)
            return carry
        lax.fori_loop(0, nsub, body, 0)

    def ffn_body(r, carry):
        xr = xe[rows_of(r), :]
        h = _silu(_dot(xr, wg_ref[...])) * _dot(xr, wu_ref[...])
        ye[rows_of(r), :] += _dot(h.astype(BF16), wd_ref[...])
        return carry

    lax.fori_loop(0, nsub, ffn_body, 0)

    @pl.when(f == n_f - 1)
    def _():
        comb = comb_ref[...]
        ch = comb.astype(BF16)
        cl = (comb - ch.astype(F32)).astype(BF16)
        c2 = jnp.concatenate([ch, cl], axis=0)
        lane = lax.broadcasted_iota(I32, (sub, 2 * N_EXPERTS), 1)
        pick = (lane == e) | (lane == e + N_EXPERTS)

        def body(r, carry):
            p = onehot(r)
            gc = lax.dot_general(p, c2, NT_DIMS, preferred_element_type=F32)
            gate = jnp.sum(jnp.where(pick, gc, 0.0), axis=1, keepdims=True)
            yw = (ye[rows_of(r), :] * gate).astype(BF16)
            o_ref[...] += lax.dot_general(p, yw, TN_DIMS, preferred_element_type=F32)
            return carry
        lax.fori_loop(0, nsub, body, 0)

    @pl.when((e == n_e - 1) & (f == n_f - 1))
    def _():
        o_ref[...] = _layer_norm(alpha * x_ref[...] + o_ref[...], g_ref[...], b_ref[...])


def _moe_ffn_ln(x2d, counts, dest_t, comb_t, wg, wu, wd, g, b, alpha, tb):
    n, d = x2d.shape
    n_e, _, d_ff = wg.shape
    fc = 512
    sub = 128
    grid_spec = pltpu.PrefetchScalarGridSpec(
        num_scalar_prefetch=1,
        grid=(n // tb, n_e, d_ff // fc),
        in_specs=[pl.BlockSpec((tb, d), lambda i, e, f, cnt: (i, 0)),
                  pl.BlockSpec((n_e, tb), lambda i, e, f, cnt: (0, i)),
                  pl.BlockSpec((n_e, tb), lambda i, e, f, cnt: (0, i)),
                  pl.BlockSpec((None, d, fc), lambda i, e, f, cnt: (e, 0, f)),
                  pl.BlockSpec((None, d, fc), lambda i, e, f, cnt: (e, 0, f)),
                  pl.BlockSpec((None, fc, d), lambda i, e, f, cnt: (e, f, 0)),
                  pl.BlockSpec((1, d), lambda i, e, f, cnt: (0, 0)),
                  pl.BlockSpec((1, d), lambda i, e, f, cnt: (0, 0))],
        out_specs=pl.BlockSpec((tb, d), lambda i, e, f, cnt: (i, 0)),
        scratch_shapes=[pltpu.VMEM((tb, d), BF16), pltpu.VMEM((tb, d), BF16), pltpu.VMEM((tb, d), F32)],
    )
    return pl.pallas_call(
        functools.partial(_moe_kernel, sub=sub, alpha=alpha),
        grid_spec=grid_spec,
        out_shape=jax.ShapeDtypeStruct((n, d), F32),
        compiler_params=_cparams(3),
        name="moe_ffn_ln",
    )(counts, x2d, dest_t, comb_t, wg, wu, wd, g, b)


def _rope_tables(s):
    half = AT_DIM // 8
    inv = ROPE_THETA ** (-jnp.arange(half, dtype=F32) / half)
    ang = jnp.arange(s).astype(F32)[:, None] * inv[None, :]
    cos = jnp.cos(ang)
    sin = jnp.sin(ang)
    ones = jnp.ones((s, AT_DIM - 2 * half), F32)
    zeros = jnp.zeros((s, AT_DIM - 2 * half), F32)
    z8 = jnp.zeros((s, half), F32)
    cos64 = jnp.concatenate([cos, cos, ones], axis=1)
    sinp64 = jnp.concatenate([z8, sin, zeros], axis=1)
    sinm64 = jnp.concatenate([-sin, z8, zeros], axis=1)
    rep = LANES // AT_DIM
    return (jnp.tile(cos64, (1, rep)), jnp.tile(sinp64, (1, rep)), jnp.tile(sinm64, (1, rep)))


def _permute_w_in(w):
    n_in = w.shape[1]
    n_ml = 4 * ML_WIDTH
    n_gate = 2 * ML_HEADS
    src = np.full((PROJ_PAD,), -1, np.int32)
    src[:n_ml] = np.arange(n_ml)
    src[n_ml:n_in - n_gate] = np.arange(n_ml + n_gate, n_in)
    src[n_in - n_gate:n_in] = np.arange(n_ml, n_ml + n_gate)
    sel = (jnp.arange(n_in, dtype=I32)[:, None] == jnp.asarray(src)[None, :]).astype(BF16)
    return jnp.dot(w.astype(BF16), sel, preferred_element_type=F32).astype(BF16)


def kernel(x, w_in, ml_conv_w, ml_conv_b, ml_i_b, ml_f_b, ml_norm_g, idx_k_norm_g, idx_k_norm_b, w_out, ln1_g, ln1_b, ln2_g, ln2_b, ffn_w_gate, ffn_w_up, ffn_w_down, moe_w_router, moe_b_router, moe_w_gate, moe_w_up, moe_w_down):
    bsz, s, d = x.shape
    depth = w_in.shape[0]
    alpha = float((2 * depth) ** 0.25)
    n = bsz * s
    cos, sinp, sinm = _rope_tables(s)
    ml_chunk = min(256, s)
    moe_tb = min(1024, n)
    zpad = lambda v, left: jnp.pad(v, (left, LANES - left - v.shape[0]))[None, :]

    x2d = x.reshape(n, d)
    for l in range(depth):
        proj = _in_proj(x2d, _permute_w_in(w_in[l]).astype(BF16)).reshape(bsz, s, PROJ_PAD)
        gate_b = zpad(jnp.concatenate([ml_i_b[l], ml_f_b[l]]), SM_MI)
        ml_out = _mlstm(proj, ml_conv_w[l], ml_conv_b[l][None, :], gate_b, ml_norm_g[l][None, :], ml_chunk)
        q_r, k_r, v_t, iq_r, ik_r, iw_t = _dsa_prep(proj, cos, sinp, sinm,
                                                    zpad(idx_k_norm_g[l], 0), zpad(idx_k_norm_b[l], 0))
        at_out = _dsa_attention(q_r, k_r, v_t, iq_r, ik_r, iw_t)
        wo = w_out[l].astype(BF16)
        x2d = _out_proj_ln(ml_out.reshape(n, ML_WIDTH), at_out.reshape(n, AT_WIDTH), x2d,
                           wo[:ML_WIDTH], wo[ML_WIDTH:], ln1_g[l][None, :], ln1_b[l][None, :], alpha)
        j = l // 2
        if l % 2 == 0:
            x2d = _dense_ffn_ln(x2d, ffn_w_gate[j].astype(BF16), ffn_w_up[j].astype(BF16),
                                ffn_w_down[j].astype(BF16), ln2_g[l][None, :], ln2_b[l][None, :], alpha)
        else:
            wr_t = moe_w_router[j].T
            wr_hi = wr_t.astype(BF16)
            wr_lo = (wr_t - wr_hi.astype(F32)).astype(BF16)
            br = jnp.broadcast_to(moe_b_router[j][:, None], (N_EXPERTS, LANES))
            comb_t, dest_t, cnt = _router(x2d, wr_hi, wr_lo, br, moe_tb)
            counts = cnt[:, :, 0].reshape(-1)
            x2d = _moe_ffn_ln(x2d, counts, dest_t, comb_t, moe_w_gate[j].astype(BF16), moe_w_up[j].astype(BF16),
                              moe_w_down[j].astype(BF16), ln2_g[l][None, :], ln2_b[l][None, :], alpha, moe_tb)
    return x2d.reshape(bsz, s, d)
```

```python
import functools

import jax
import jax.numpy as jnp
import numpy as np
from jax import lax
from jax.experimental import pallas as pl
from jax.experimental.pallas import tpu as pltpu

F32 = jnp.float32
BF16 = jnp.bfloat16
I32 = jnp.int32

ML_HEADS = 4
ML_DIM = 128
ML_WIDTH = ML_HEADS * ML_DIM
CONV_W = 4
AT_HEADS = 8
AT_KV_HEADS = 2
AT_DIM = 64
AT_WIDTH = AT_HEADS * AT_DIM
KV_WIDTH = AT_KV_HEADS * AT_DIM
IDX_HEADS = 4
IDX_DIM = 64
TOPK_MAX = 256
ROPE_THETA = 500000.0
N_EXPERTS = 8
LN_EPS = 1e-5

LANES = 128
SUBLANES = 8
VMEM_LIMIT_BYTES = 56 * 1024 * 1024

COL_MQ = 0
COL_MK = 512
COL_MV = 1024
COL_MO = 1536
COL_AQ = 2048
COL_AKV = 2560
COL_IQ = 2816
COL_SMALL = 3072
PROJ_PAD = 3200
SM_IW = 64
SM_MI = 68
SM_MF = 72

MOE_TOKEN_CHUNK = 2 * LANES
MOE_ROW_TILE = LANES

NEG_BIG = -1e30
BF16_INF_PATTERN = 0x7F80
NT_DIMS = (((1,), (1,)), ((), ()))
TN_DIMS = (((0,), (0,)), ((), ()))


def _cparams(n_axes):
    return pltpu.CompilerParams(dimension_semantics=("arbitrary",) * n_axes,
                                vmem_limit_bytes=VMEM_LIMIT_BYTES)


def _dot(a, b):
    return jnp.dot(a, b, preferred_element_type=F32)


def _layer_norm(z, g, b):
    mu = jnp.mean(z, axis=-1, keepdims=True)
    d = z - mu
    var = jnp.mean(d * d, axis=-1, keepdims=True)
    return d * lax.rsqrt(var + LN_EPS) * g + b


def _silu(x):
    return x / (1.0 + jnp.exp(-x))


def _in_proj_kernel(x_ref, w_ref, o_ref):
    o_ref[...] = _dot(x_ref[...].astype(BF16), w_ref[...])


def _in_proj(x2d, w_bf):
    n, d = x2d.shape
    pw = w_bf.shape[1]
    tm = 512
    return pl.pallas_call(
        _in_proj_kernel,
        grid=(n // tm,),
        in_specs=[pl.BlockSpec((tm, d), lambda i: (i, 0)),
                  pl.BlockSpec((d, pw), lambda i: (0, 0))],
        out_specs=pl.BlockSpec((tm, pw), lambda i: (i, 0)),
        out_shape=jax.ShapeDtypeStruct((n, pw), F32),
        compiler_params=_cparams(1),
        name="in_proj",
    )(x2d, w_bf)


def _mlstm_kernel(q_ref, k_ref, v_ref, o_ref, sm_ref, cw_ref, cb_ref, gb_ref, ng_ref, out_ref,
                  c_st, n_st, m_st, tail):
    chunk = q_ref.shape[0]
    ci = pl.program_id(1)

    @pl.when(ci == 0)
    def _():
        c_st[...] = jnp.zeros_like(c_st)
        n_st[...] = jnp.zeros_like(n_st)
        m_st[...] = jnp.zeros_like(m_st)
        tail[...] = jnp.zeros_like(tail)

    def conv_silu(x, prev, w, b):
        cat = jnp.concatenate([prev, x], axis=0)
        y = pltpu.roll(cat, 3, 0)[SUBLANES:] * w[0:1]
        y = y + pltpu.roll(cat, 2, 0)[SUBLANES:] * w[1:2]
        y = y + pltpu.roll(cat, 1, 0)[SUBLANES:] * w[2:3]
        y = y + x * w[3:4]
        return _silu(y + b)

    xq = q_ref[...]
    xk = k_ref[...]
    cw = cw_ref[...]
    cb = cb_ref[...]
    qs = conv_silu(xq, tail[:, :ML_WIDTH], cw[:, :ML_WIDTH], cb[:, :ML_WIDTH]) * (ML_DIM ** -0.5)
    ks = conv_silu(xk, tail[:, ML_WIDTH:], cw[:, ML_WIDTH:], cb[:, ML_WIDTH:])
    tail[:, :ML_WIDTH] = xq[chunk - SUBLANES:]
    tail[:, ML_WIDTH:] = xk[chunk - SUBLANES:]

    gates = sm_ref[...] + gb_ref[...]
    logf = jnp.minimum(gates, 0.0) - jnp.log1p(jnp.exp(-jnp.abs(gates)))
    row = lax.broadcasted_iota(I32, (chunk, chunk), 0)
    col = lax.broadcasted_iota(I32, (chunk, chunk), 1)
    causal = row >= col
    tri = jnp.where(causal, 1.0, 0.0).astype(BF16)
    hi = logf.astype(BF16)
    r1 = logf - hi.astype(F32)
    mid = r1.astype(BF16)
    lo = (r1 - mid.astype(F32)).astype(BF16)
    bcum = _dot(tri, hi) + _dot(tri, mid) + _dot(tri, lo)
    gates_t = gates.T
    bcum_t = bcum.T

    for h in range(ML_HEADS):
        sl = slice(h * ML_DIM, (h + 1) * ML_DIM)
        qh = qs[:, sl]
        kh = ks[:, sl]
        vb = v_ref[:, sl].astype(BF16)
        b = bcum[:, SM_MF + h:SM_MF + h + 1]
        li = gates[:, SM_MI + h:SM_MI + h + 1]
        r = gates_t[SM_MI + h:SM_MI + h + 1, :] - bcum_t[SM_MF + h:SM_MF + h + 1, :]
        m_prev = m_st[h][0:1, 0:1]
        dmat = jnp.where(causal, b + r, -jnp.inf)
        g_inter = b + m_prev
        m_t = jnp.maximum(g_inter, jnp.max(dmat, axis=1, keepdims=True))
        w_inter = jnp.exp(g_inter - m_t)
        qb = qh.astype(BF16)
        kb = kh.astype(BF16)
        sc = lax.dot_general(qb, kb, NT_DIMS, preferred_element_type=F32) * jnp.exp(dmat - m_t)
        c_old = c_st[h]
        n_old = n_st[h][0:1, :]
        num = w_inter * _dot(qb, c_old.astype(BF16)) + _dot(sc.astype(BF16), vb)
        den = w_inter * jnp.sum(qh * n_old, axis=1, keepdims=True) + jnp.sum(sc, axis=1, keepdims=True)
        hh = num / jnp.maximum(jnp.abs(den), jnp.exp(-m_t))
        b_last = b[chunk - 1:chunk, :]
        g_state = b_last + m_prev
        ls = b_last - b + li
        m_new = jnp.maximum(g_state, jnp.max(ls, axis=0, keepdims=True))
        ws = jnp.exp(ls - m_new)
        decay = jnp.exp(g_state - m_new)
        kw = kh * ws
        c_st[h] = decay * c_old + _dot(kw.T.astype(BF16), vb)
        n_st[h] = jnp.broadcast_to(decay * n_old + jnp.sum(kw, axis=0, keepdims=True), (SUBLANES, ML_DIM))
        m_st[h] = jnp.broadcast_to(m_new, (SUBLANES, LANES))
        mu = jnp.mean(hh, axis=1, keepdims=True)
        d = hh - mu
        var = jnp.mean(d * d, axis=1, keepdims=True)
        hn = d * lax.rsqrt(var + LN_EPS) * ng_ref[:, sl]
        out_ref[:, sl] = (hn / (1.0 + jnp.exp(-o_ref[:, sl]))).astype(out_ref.dtype)


def _mlstm(proj, conv_w, conv_b, gate_b, norm_g, chunk):
    bsz, s, _ = proj.shape
    wblk = ML_WIDTH

    def colspec(col, width):
        return pl.BlockSpec((None, chunk, width), lambda b, c: (b, c, col // width))

    def full2d(shape):
        return pl.BlockSpec(shape, lambda b, c: (0, 0))

    return pl.pallas_call(
        _mlstm_kernel,
        grid=(bsz, s // chunk),
        in_specs=[colspec(COL_MQ, wblk), colspec(COL_MK, wblk), colspec(COL_MV, wblk), colspec(COL_MO, wblk),
                  colspec(COL_SMALL, LANES),
                  full2d((CONV_W, 2 * ML_WIDTH)), full2d((1, 2 * ML_WIDTH)), full2d((1, LANES)),
                  full2d((1, ML_WIDTH))],
        out_specs=pl.BlockSpec((None, chunk, ML_WIDTH), lambda b, c: (b, c, 0)),
        out_shape=jax.ShapeDtypeStruct((bsz, s, ML_WIDTH), BF16),
        scratch_shapes=[pltpu.VMEM((ML_HEADS, ML_DIM, ML_DIM), F32),
                        pltpu.VMEM((ML_HEADS, SUBLANES, ML_DIM), F32),
                        pltpu.VMEM((ML_HEADS, SUBLANES, LANES), F32),
                        pltpu.VMEM((SUBLANES, 2 * ML_WIDTH), F32)],
        compiler_params=_cparams(2),
        name="mlstm",
    )(proj, proj, proj, proj, proj, conv_w, conv_b, gate_b, norm_g)


def _rope(x, cos, sinp, sinm):
    width = x.shape[1]
    reps = width // LANES
    if reps > 1:
        cos = jnp.concatenate([cos] * reps, axis=1)
        sinp = jnp.concatenate([sinp] * reps, axis=1)
        sinm = jnp.concatenate([sinm] * reps, axis=1)
    half = AT_DIM // 8
    return x * cos + pltpu.roll(x, half, 1) * sinp + pltpu.roll(x, width - half, 1) * sinm


def _dsa_prep_kernel(aq_ref, akv_ref, iq_ref, sm_ref, cos_ref, sinp_ref, sinm_ref, lng_ref, lnb_ref,
                     qt_out, k_out, vt_out, iqt_out, ik_out, iwt_out):
    cos = cos_ref[...]
    sinp = sinp_ref[...]
    sinm = sinm_ref[...]
    q = _rope(aq_ref[...], cos, sinp, sinm) * (AT_DIM ** -0.5)
    sm = sm_ref[...]
    lane = lax.broadcasted_iota(I32, sm.shape, 1)
    rep = AT_HEADS // AT_KV_HEADS
    for pair in range(AT_HEADS // 2):
        slab = q[:, pair * LANES:(pair + 1) * LANES]
        swapped = pltpu.roll(slab, AT_DIM, 1)
        for p in range(2):
            h = 2 * pair + p
            g = h // rep
            src = slab if p == g else swapped
            in_group = (lane >= g * AT_DIM) & (lane < (g + 1) * AT_DIM)
            qt_out[h * LANES:(h + 1) * LANES, :] = jnp.where(in_group, src, 0.0).T.astype(BF16)
    akv = akv_ref[...]
    k_out[...] = _rope(akv[:, :KV_WIDTH], cos, sinp, sinm).astype(BF16)
    vt_out[...] = akv[:, KV_WIDTH:].T.astype(BF16)
    iqt_out[...] = (_rope(iq_ref[...], cos, sinp, sinm) * (IDX_DIM ** -0.5)).T.astype(BF16)
    is_k = lane < IDX_DIM
    mu = jnp.sum(jnp.where(is_k, sm, 0.0), axis=1, keepdims=True) * (1.0 / IDX_DIM)
    d = jnp.where(is_k, sm - mu, 0.0)
    var = jnp.sum(d * d, axis=1, keepdims=True) * (1.0 / IDX_DIM)
    y = jnp.where(is_k, d * lax.rsqrt(var + LN_EPS) * lng_ref[...] + lnb_ref[...], 0.0)
    ik_out[...] = _rope(y, cos, sinp, sinm).astype(BF16)
    iwt_out[...] = sm.T[SM_IW:SM_IW + SUBLANES, :] * (IDX_HEADS ** -0.5)


def _dsa_prep(proj, cos, sinp, sinm, ln_g, ln_b):
    bsz, s, _ = proj.shape
    t = 512

    def colspec(col, width):
        return pl.BlockSpec((None, t, width), lambda b, c: (b, c, col // width))

    tab = pl.BlockSpec((t, LANES), lambda b, c: (c, 0))
    vec = pl.BlockSpec((1, LANES), lambda b, c: (0, 0))

    def outspec(width):
        return pl.BlockSpec((None, t, width), lambda b, c: (b, c, 0))

    return pl.pallas_call(
        _dsa_prep_kernel,
        grid=(bsz, s // t),
        in_specs=[colspec(COL_AQ, AT_WIDTH), colspec(COL_AKV, 2 * KV_WIDTH), colspec(COL_IQ, IDX_HEADS * IDX_DIM),
                  colspec(COL_SMALL, LANES), tab, tab, tab, vec, vec],
        out_specs=[pl.BlockSpec((None, AT_HEADS * LANES, t), lambda b, c: (b, 0, c)),
                   outspec(KV_WIDTH),
                   pl.BlockSpec((None, KV_WIDTH, t), lambda b, c: (b, 0, c)),
                   pl.BlockSpec((None, IDX_HEADS * IDX_DIM, t), lambda b, c: (b, 0, c)),
                   outspec(LANES),
                   pl.BlockSpec((None, SUBLANES, t), lambda b, c: (b, 0, c))],
        out_shape=[jax.ShapeDtypeStruct((bsz, AT_HEADS * LANES, s), BF16),
                   jax.ShapeDtypeStruct((bsz, s, KV_WIDTH), BF16),
                   jax.ShapeDtypeStruct((bsz, KV_WIDTH, s), BF16),
                   jax.ShapeDtypeStruct((bsz, IDX_HEADS * IDX_DIM, s), BF16),
                   jax.ShapeDtypeStruct((bsz, s, LANES), BF16),
                   jax.ShapeDtypeStruct((bsz, SUBLANES, s), F32)],
        compiler_params=_cparams(2),
        name="dsa_prep",
    )(proj, proj, proj, proj, cos, sinp, sinm, ln_g, ln_b)


def _tree_sum(parts):
    while len(parts) > 1:
        parts = [parts[i] + parts[i + 1] for i in range(0, len(parts) - 1, 2)] + ([parts[-1]] if len(parts) % 2 else [])
    return parts[0]


def _dsa_kernel(qt_ref, iqt_ref, iwt_ref, k_ref, vt_ref, ik_ref, out_ref, score, coarse, bias, acc, m_s, l_s, x_s,
                *, topk, seq, ck):
    tq = qt_ref.shape[1]
    j = pl.program_id(1)
    nck = (j * tq + tq + ck - 1) // ck
    t_abs = j * tq + lax.broadcasted_iota(I32, (ck, tq), 1)
    s_loc = lax.broadcasted_iota(I32, (ck, tq), 0)
    rep = AT_HEADS // AT_KV_HEADS
    pack = 2 * SUBLANES

    iqt = iqt_ref[...]
    iwt = iwt_ref[...]
    iq_all = jnp.concatenate([iqt[h * IDX_DIM:(h + 1) * IDX_DIM, :] for h in range(IDX_HEADS)], axis=1)
    iw_all = jnp.concatenate([iwt[h:h + 1, :] for h in range(IDX_HEADS)], axis=1)

    def chunk_off(c):
        return pl.multiple_of(c * ck, ck)

    def score_body(c, carry):
        off = chunk_off(c)
        kic = ik_ref[pl.ds(off, ck), :][:, :IDX_DIM]
        sw = jnp.maximum(_dot(kic, iq_all), 0.0) * iw_all
        s = sw[:, :tq]
        for h in range(1, IDX_HEADS):
            s = s + sw[:, h * tq:(h + 1) * tq]
        s = jnp.where(off + s_loc <= t_abs, s, -jnp.inf)
        s = jnp.where(s == 0.0, 0.0, s)
        score[pl.ds(off, ck), :] = s
        coarse[pl.ds(off, ck), :] = s.astype(BF16)
        return carry

    lax.fori_loop(0, nck, score_body, 0)

    def count(pred):
        def body(c, a):
            off = chunk_off(c)
            m = jnp.where(pred(score[pl.ds(off, ck), :], off), 1, 0)
            return a + jnp.sum(m.reshape(ck // SUBLANES, SUBLANES, tq), axis=0)
        a = lax.fori_loop(0, nck, body, jnp.zeros((SUBLANES, tq), I32))
        return jnp.sum(a, axis=0, keepdims=True)

    def count_coarse(pred):
        one = jnp.ones((), BF16)
        zero = jnp.zeros((), BF16)

        def body(c, a):
            m = jnp.where(pred(coarse[pl.ds(chunk_off(c), ck), :]), one, zero).reshape(ck // pack, pack, tq)
            return a + _tree_sum([m[i] for i in range(ck // pack)]).astype(F32)
        a = lax.fori_loop(0, nck, body, jnp.zeros((pack, tq), F32))
        return jnp.sum(a, axis=0, keepdims=True).astype(I32)

    def key_to_f32(k):
        return pltpu.bitcast(k ^ ((k >> 31) & jnp.int32(0x7FFFFFFF)), F32)

    def key16_to_bf16(k):
        pattern = (k ^ ((k >> 15) & 0x7FFF)) & 0xFFFF
        return pltpu.bitcast(pattern << 16, F32).astype(BF16)

    lo16 = -(1 << 15)
    t16 = jnp.full((1, tq), lo16, I32)
    for bit in reversed(range(16)):
        cand = t16 + (1 << bit)
        cand_b = key16_to_bf16(cand)
        c = count_coarse(lambda hc, cand_b=cand_b: hc >= cand_b)
        t16 = jnp.where(c >= topk, cand, t16)
    key_lo = jnp.clip(t16 - 1, lo16, BF16_INF_PATTERN - 1) << 16
    offs = jnp.zeros((1, tq), I32)
    for bit in reversed(range(18)):
        cand = offs + (1 << bit)
        cand_f = key_to_f32(key_lo + cand)
        c = count(lambda sc, off, cand_f=cand_f: sc >= cand_f)
        offs = jnp.where(c >= topk, cand, offs)
    thr = key_to_f32(key_lo + offs)

    take_all = t_abs[0:1, :] < topk
    n_gt = count(lambda sc, off: sc > thr)
    n_eq = count(lambda sc, off: sc == thr)
    need = topk - n_gt
    excess = (n_eq > need) & jnp.logical_not(take_all)
    x_s[...] = jnp.full(x_s.shape, seq, I32)

    @pl.when(jnp.max(jnp.where(excess, 1, 0)) > 0)
    def _():
        x = jnp.zeros((1, tq), I32)
        for bit in reversed(range(max(seq - 1, 1).bit_length())):
            cand = x + (1 << bit)
            c = count(lambda sc, off: (sc == thr) & (off + s_loc < cand))
            x = jnp.where(c < need, cand, x)
        x_s[...] = jnp.broadcast_to(jnp.where(excess, x, seq), x_s.shape)

    xlim = x_s[0:1, :]

    def bias_body(c, carry):
        off = chunk_off(c)
        sc = score[pl.ds(off, ck), :]
        s_abs = off + s_loc
        sel = (sc > thr) | ((sc == thr) & (s_abs <= xlim)) | take_all
        bias[pl.ds(off, ck), :] = jnp.where(sel & (s_abs <= t_abs), 0.0, NEG_BIG)
        return carry

    lax.fori_loop(0, nck, bias_body, 0)

    m_s[...] = jnp.full(m_s.shape, NEG_BIG, F32)
    l_s[...] = jnp.zeros_like(l_s)
    acc[...] = jnp.zeros_like(acc)

    def att_body(c, carry):
        off = chunk_off(c)
        kc = k_ref[pl.ds(off, ck), :]
        vtc = vt_ref[:, pl.ds(off, ck)]
        bc = bias[pl.ds(off, ck), :]
        b_all = jnp.concatenate([bc] * rep, axis=1)
        for g in range(AT_KV_HEADS):
            qg = jnp.concatenate([qt_ref[(g * rep + r) * LANES:(g * rep + r + 1) * LANES, :] for r in range(rep)],
                                 axis=1)
            lg = _dot(kc, qg) + b_all
            m_old = m_s[g]
            m_new = jnp.maximum(m_old, jnp.max(lg, axis=0, keepdims=True))
            a = jnp.exp(m_old - m_new)
            p = jnp.exp(lg - m_new)
            l_s[g] = a * l_s[g] + jnp.sum(p, axis=0, keepdims=True)
            acc[g] = a * acc[g] + _dot(vtc, p.astype(BF16))
            m_s[g] = m_new
        return carry

    lax.fori_loop(0, nck, att_body, 0)

    lane = lax.broadcasted_iota(I32, (tq, LANES), 1)
    for g in range(AT_KV_HEADS):
        on = acc[g] / l_s[g]
        for pp in range(rep // 2):
            t0 = on[:, 2 * pp * tq:(2 * pp + 1) * tq].T
            t1 = on[:, (2 * pp + 1) * tq:(2 * pp + 2) * tq].T
            if g == 0:
                slab = jnp.where(lane < AT_DIM, t0, pltpu.roll(t1, AT_DIM, 1))
            else:
                slab = jnp.where(lane < AT_DIM, pltpu.roll(t0, AT_DIM, 1), t1)
            pair = g * (rep // 2) + pp
            out_ref[:, pair * LANES:(pair + 1) * LANES] = slab.astype(out_ref.dtype)


def _dsa_attention(q_t, k_r, v_t, iq_t, ik_r, iw_t):
    bsz, s, _ = k_r.shape
    tq = min(256, s)
    topk = min(TOPK_MAX, s // 4)
    rep = AT_HEADS // AT_KV_HEADS
    ck = min(512, s)
    assert s % ck == 0 and ck >= topk and ck % tq == 0
    kern = functools.partial(_dsa_kernel, topk=topk, seq=s, ck=ck)
    return pl.pallas_call(
        kern,
        grid=(bsz, s // tq),
        in_specs=[pl.BlockSpec((None, AT_HEADS * LANES, tq), lambda b, j: (b, 0, j)),
                  pl.BlockSpec((None, IDX_HEADS * IDX_DIM, tq), lambda b, j: (b, 0, j)),
                  pl.BlockSpec((None, SUBLANES, tq), lambda b, j: (b, 0, j)),
                  pl.BlockSpec((None, s, KV_WIDTH), lambda b, j: (b, 0, 0)),
                  pl.BlockSpec((None, KV_WIDTH, s), lambda b, j: (b, 0, 0)),
                  pl.BlockSpec((None, s, LANES), lambda b, j: (b, 0, 0))],
        out_specs=pl.BlockSpec((None, tq, AT_WIDTH), lambda b, j: (b, j, 0)),
        out_shape=jax.ShapeDtypeStruct((bsz, s, AT_WIDTH), BF16),
        scratch_shapes=[pltpu.VMEM((s, tq), F32),
                        pltpu.VMEM((s, tq), BF16),
                        pltpu.VMEM((s, tq), F32),
                        pltpu.VMEM((AT_KV_HEADS, LANES, rep * tq), F32),
                        pltpu.VMEM((AT_KV_HEADS, 1, rep * tq), F32),
                        pltpu.VMEM((AT_KV_HEADS, 1, rep * tq), F32),
                        pltpu.VMEM((SUBLANES, tq), I32)],
        compiler_params=_cparams(2),
        name="dsa_attn",
    )(q_t, iq_t, iw_t, k_r, v_t, ik_r)


def _out_ln_kernel(ml_ref, at_ref, x_ref, wt_ref, wb_ref, g_ref, b_ref, o_ref, *, alpha):
    mix = _dot(ml_ref[...], wt_ref[...]) + _dot(at_ref[...], wb_ref[...])
    o_ref[...] = _layer_norm(alpha * x_ref[...] + mix, g_ref[...], b_ref[...])


def _out_proj_ln(ml, at, x2d, w_top, w_bot, g, b, alpha):
    n, d = x2d.shape
    tm = 512
    row = lambda i: (i, 0)
    const = lambda i: (0, 0)
    return pl.pallas_call(
        functools.partial(_out_ln_kernel, alpha=alpha),
        grid=(n // tm,),
        in_specs=[pl.BlockSpec((tm, ML_WIDTH), row), pl.BlockSpec((tm, AT_WIDTH), row), pl.BlockSpec((tm, d), row),
                  pl.BlockSpec((ML_WIDTH, d), const), pl.BlockSpec((AT_WIDTH, d), const),
                  pl.BlockSpec((1, d), const), pl.BlockSpec((1, d), const)],
        out_specs=pl.BlockSpec((tm, d), row),
        out_shape=jax.ShapeDtypeStruct((n, d), F32),
        compiler_params=_cparams(1),
        name="out_proj_ln",
    )(ml, at, x2d, w_top, w_bot, g, b)


def _ffn_kernel(x_ref, wg_ref, wu_ref, wd_ref, g_ref, b_ref, o_ref, acc_ref, xb_ref, *, alpha):
    f = pl.program_id(1)

    @pl.when(f == 0)
    def _():
        xb_ref[...] = x_ref[...].astype(BF16)
        acc_ref[...] = jnp.zeros_like(acc_ref)

    xb = xb_ref[...]
    h = _silu(_dot(xb, wg_ref[...])) * _dot(xb, wu_ref[...])
    acc_ref[...] += _dot(h.astype(BF16), wd_ref[...])

    @pl.when(f == pl.num_programs(1) - 1)
    def _():
        o_ref[...] = _layer_norm(alpha * x_ref[...] + acc_ref[...], g_ref[...], b_ref[...])


def _ffn_chunk(d_ff):
    best = LANES
    for c in range(LANES, 1408 + 1, LANES):
        if d_ff % c == 0:
            best = c
    return best


def _dense_ffn_ln(x2d, wg, wu, wd, g, b, alpha):
    n, d = x2d.shape
    d_ff = wg.shape[1]
    tm = 512
    fc = _ffn_chunk(d_ff)
    return pl.pallas_call(
        functools.partial(_ffn_kernel, alpha=alpha),
        grid=(n // tm, d_ff // fc),
        in_specs=[pl.BlockSpec((tm, d), lambda i, f: (i, 0)),
                  pl.BlockSpec((d, fc), lambda i, f: (0, f)),
                  pl.BlockSpec((d, fc), lambda i, f: (0, f)),
                  pl.BlockSpec((fc, d), lambda i, f: (f, 0)),
                  pl.BlockSpec((1, d), lambda i, f: (0, 0)),
                  pl.BlockSpec((1, d), lambda i, f: (0, 0))],
        out_specs=pl.BlockSpec((tm, d), lambda i, f: (i, 0)),
        out_shape=jax.ShapeDtypeStruct((n, d), F32),
        scratch_shapes=[pltpu.VMEM((tm, d), F32), pltpu.VMEM((tm, d), BF16)],
        compiler_params=_cparams(2),
        name="dense_ffn_ln",
    )(x2d, wg, wu, wd, g, b)


def _router_kernel(x_ref, wh_ref, wl_ref, br_ref, comb_ref, dest_ref, cnt_ref):
    tb = x_ref.shape[0]
    x = x_ref[...]
    xh = x.astype(BF16)
    xl = (x - xh.astype(F32)).astype(BF16)
    wh = wh_ref[...]
    lg = (lax.dot_general(wh, xh, NT_DIMS, preferred_element_type=F32)
          + lax.dot_general(wh, xl, NT_DIMS, preferred_element_type=F32)
          + lax.dot_general(wl_ref[...], xh, NT_DIMS, preferred_element_type=F32)) + br_ref[:, 0:1]
    e_id = lax.broadcasted_iota(I32, (N_EXPERTS, tb), 0)
    m1 = jnp.max(lg, axis=0, keepdims=True)
    i1 = jnp.min(jnp.where(lg == m1, e_id, N_EXPERTS), axis=0, keepdims=True)
    lg2 = jnp.where(e_id == i1, -jnp.inf, lg)
    m2 = jnp.max(lg2, axis=0, keepdims=True)
    i2 = jnp.min(jnp.where(lg2 == m2, e_id, N_EXPERTS), axis=0, keepdims=True)
    ex = jnp.exp(m2 - m1)
    g1 = 1.0 / (1.0 + ex)
    g2 = ex / (1.0 + ex)
    comb = jnp.where(e_id == i1, g1, 0.0) + jnp.where(e_id == i2, g2, 0.0)
    comb_ref[...] = comb
    mask = jnp.where((e_id == i1) | (e_id == i2), 1.0, 0.0)
    w = MOE_TOKEN_CHUNK
    upper = jnp.where(lax.broadcasted_iota(I32, (w, w), 0) < lax.broadcasted_iota(I32, (w, w), 1), 1.0, 0.0).astype(BF16)
    lane = lax.broadcasted_iota(I32, (N_EXPERTS, LANES), 1)
    cnt = jnp.zeros((N_EXPERTS, LANES), F32)
    off = jnp.zeros((N_EXPERTS, 1), F32)
    for c in range(tb // w):
        cnt = jnp.where(lane == c, off, cnt)
        mc = mask[:, c * w:(c + 1) * w]
        dest_ref[:, c * w:(c + 1) * w] = (_dot(mc.astype(BF16), upper) + off).astype(I32)
        off = off + jnp.sum(mc, axis=1, keepdims=True)
    cnt_ref[...] = jnp.where(lane == tb // w, off, cnt).astype(I32)


def _router(x2d, w_hi, w_lo, b_r, tb):
    n, d = x2d.shape
    nb = n // tb
    return pl.pallas_call(
        _router_kernel,
        grid=(nb,),
        in_specs=[pl.BlockSpec((tb, d), lambda i: (i, 0)),
                  pl.BlockSpec((N_EXPERTS, d), lambda i: (0, 0)),
                  pl.BlockSpec((N_EXPERTS, d), lambda i: (0, 0)),
                  pl.BlockSpec((N_EXPERTS, LANES), lambda i: (0, 0))],
        out_specs=[pl.BlockSpec((N_EXPERTS, tb), lambda i: (0, i)),
                   pl.BlockSpec((N_EXPERTS, tb), lambda i: (0, i)),
                   pl.BlockSpec((None, N_EXPERTS, LANES), lambda i: (i, 0, 0))],
        out_shape=[jax.ShapeDtypeStruct((N_EXPERTS, n), F32),
                   jax.ShapeDtypeStruct((N_EXPERTS, n), I32),
                   jax.ShapeDtypeStruct((nb, N_EXPERTS, LANES), I32)],
        compiler_params=_cparams(1),
        name="moe_router",
    )(x2d, w_hi, w_lo, b_r)


def _moe_kernel(cnt_ref, x_ref, dest_ref, comb_ref, wg_ref, wu_ref, wd_ref, g_ref, b_ref, o_ref,
                xe, ye, gs, *, alpha):
    sub = MOE_ROW_TILE
    cw = MOE_TOKEN_CHUNK
    tb, d = x_ref.shape
    nch = tb // cw
    i = pl.program_id(0)
    e = pl.program_id(1)
    f = pl.program_id(2)
    n_e = pl.num_programs(1)
    n_f = pl.num_programs(2)
    base = (i * n_e + e) * (nch + 1)
    cum = [cnt_ref[base + k] for k in range(nch + 1)]
    nsub = (cum[nch] + sub - 1) // sub

    @pl.when((e == 0) & (f == 0))
    def _():
        o_ref[...] = jnp.zeros_like(o_ref)

    def onehot(r0, ch):
        cols = slice(ch * cw, (ch + 1) * cw)
        drow = dest_ref[pl.ds(e, 1), cols]
        crow = comb_ref[pl.ds(e, 1), cols]
        slot = r0 + lax.broadcasted_iota(I32, (sub, cw), 0)
        return jnp.where((drow == slot) & (crow > 0.0), 1.0, 0.0).astype(BF16)

    def for_overlapping_chunks(r0, fn):
        for ch in range(nch):
            pl.when((cum[ch] < r0 + sub) & (cum[ch + 1] > r0))(functools.partial(fn, ch))

    @pl.when(f == 0)
    def _():
        def body(r, carry):
            r0 = pl.multiple_of(r * sub, sub)
            rows = pl.ds(r0, sub)
            xe[rows, :] = jnp.zeros((sub, d), BF16)
            ye[rows, :] = jnp.zeros((sub, d), F32)
            gs[rows, :] = jnp.zeros((sub, LANES), F32)

            def gather(ch):
                cols = slice(ch * cw, (ch + 1) * cw)
                p = onehot(r0, ch)
                xe[rows, :] += _dot(p, x_ref[cols, :].astype(BF16)).astype(BF16)
                crow = comb_ref[pl.ds(e, 1), cols]
                hi = crow.astype(BF16)
                lo = (crow - hi.astype(F32)).astype(BF16)
                c2 = jnp.concatenate([hi, lo, jnp.zeros((LANES - 2, cw), BF16)], axis=0)
                gs[rows, :] += lax.dot_general(p, c2, NT_DIMS, preferred_element_type=F32)

            for_overlapping_chunks(r0, gather)
            return carry
        lax.fori_loop(0, nsub, body, 0)

    def ffn_tile(r0, m):
        rows = pl.ds(r0, m)
        xr = xe[rows, :]
        h = _silu(_dot(xr, wg_ref[...])) * _dot(xr, wu_ref[...])
        ye[rows, :] += _dot(h.astype(BF16), wd_ref[...])

    def ffn_pair(r, carry):
        ffn_tile(pl.multiple_of(r * 2 * sub, 2 * sub), 2 * sub)
        return carry

    n_pair = nsub // 2
    lax.fori_loop(0, n_pair, ffn_pair, 0)

    @pl.when(nsub % 2 == 1)
    def _():
        ffn_tile(pl.multiple_of(n_pair * 2 * sub, 2 * sub), sub)

    @pl.when(f == n_f - 1)
    def _():
        def body(r, carry):
            r0 = pl.multiple_of(r * sub, sub)
            rows = pl.ds(r0, sub)
            gate = gs[rows, 0:1] + gs[rows, 1:2]
            yw = (ye[rows, :] * gate).astype(BF16)

            def scatter(ch):
                cols = slice(ch * cw, (ch + 1) * cw)
                o_ref[cols, :] += lax.dot_general(onehot(r0, ch), yw, TN_DIMS, preferred_element_type=F32)

            for_overlapping_chunks(r0, scatter)
            return carry
        lax.fori_loop(0, nsub, body, 0)

    @pl.when((e == n_e - 1) & (f == n_f - 1))
    def _():
        o_ref[...] = _layer_norm(alpha * x_ref[...] + o_ref[...], g_ref[...], b_ref[...])


def _moe_ffn_ln(x2d, counts, dest_t, comb_t, wg, wu, wd, g, b, alpha, tb):
    n, d = x2d.shape
    n_e, _, d_ff = wg.shape
    fc = 512
    grid_spec = pltpu.PrefetchScalarGridSpec(
        num_scalar_prefetch=1,
        grid=(n // tb, n_e, d_ff // fc),
        in_specs=[pl.BlockSpec((tb, d), lambda i, e, f, cnt: (i, 0), pipeline_mode=pl.Buffered(1)),
                  pl.BlockSpec((n_e, tb), lambda i, e, f, cnt: (0, i)),
                  pl.BlockSpec((n_e, tb), lambda i, e, f, cnt: (0, i)),
                  pl.BlockSpec((None, d, fc), lambda i, e, f, cnt: (e, 0, f)),
                  pl.BlockSpec((None, d, fc), lambda i, e, f, cnt: (e, 0, f)),
                  pl.BlockSpec((None, fc, d), lambda i, e, f, cnt: (e, f, 0)),
                  pl.BlockSpec((1, d), lambda i, e, f, cnt: (0, 0)),
                  pl.BlockSpec((1, d), lambda i, e, f, cnt: (0, 0))],
        out_specs=pl.BlockSpec((tb, d), lambda i, e, f, cnt: (i, 0)),
        scratch_shapes=[pltpu.VMEM((tb, d), BF16), pltpu.VMEM((tb, d), F32), pltpu.VMEM((tb, LANES), F32)],
    )
    return pl.pallas_call(
        functools.partial(_moe_kernel, alpha=alpha),
        grid_spec=grid_spec,
        out_shape=jax.ShapeDtypeStruct((n, d), F32),
        compiler_params=_cparams(3),
        name="moe_ffn_ln",
    )(counts, x2d, dest_t, comb_t, wg, wu, wd, g, b)


def _rope_tables(s):
    half = AT_DIM // 8
    inv = ROPE_THETA ** (-jnp.arange(half, dtype=F32) / half)
    ang = jnp.arange(s).astype(F32)[:, None] * inv[None, :]
    cos = jnp.cos(ang)
    sin = jnp.sin(ang)
    ones = jnp.ones((s, AT_DIM - 2 * half), F32)
    zeros = jnp.zeros((s, AT_DIM - 2 * half), F32)
    z8 = jnp.zeros((s, half), F32)
    cos64 = jnp.concatenate([cos, cos, ones], axis=1)
    sinp64 = jnp.concatenate([z8, sin, zeros], axis=1)
    sinm64 = jnp.concatenate([-sin, z8, zeros], axis=1)
    rep = LANES // AT_DIM
    return (jnp.tile(cos64, (1, rep)), jnp.tile(sinp64, (1, rep)), jnp.tile(sinm64, (1, rep)))


def _permute_w_in(w):
    n_in = w.shape[1]
    n_ml = 4 * ML_WIDTH
    n_gate = 2 * ML_HEADS
    src = np.full((PROJ_PAD,), -1, np.int32)
    src[:n_ml] = np.arange(n_ml)
    src[n_ml:n_in - n_gate] = np.arange(n_ml + n_gate, n_in)
    src[n_in - n_gate:n_in] = np.arange(n_ml, n_ml + n_gate)
    sel = (jnp.arange(n_in, dtype=I32)[:, None] == jnp.asarray(src)[None, :]).astype(BF16)
    return jnp.dot(w.astype(BF16), sel, preferred_element_type=F32).astype(BF16)


def kernel(x, w_in, ml_conv_w, ml_conv_b, ml_i_b, ml_f_b, ml_norm_g, idx_k_norm_g, idx_k_norm_b, w_out, ln1_g, ln1_b, ln2_g, ln2_b, ffn_w_gate, ffn_w_up, ffn_w_down, moe_w_router, moe_b_router, moe_w_gate, moe_w_up, moe_w_down):
    bsz, s, d = x.shape
    depth = w_in.shape[0]
    alpha = float((2 * depth) ** 0.25)
    n = bsz * s
    cos, sinp, sinm = _rope_tables(s)
    ml_chunk = min(256, s)
    moe_tb = min(2048, n)
    zpad = lambda v, left: jnp.pad(v, (left, LANES - left - v.shape[0]))[None, :]

    x2d = x.reshape(n, d)
    for l in range(depth):
        proj = _in_proj(x2d, _permute_w_in(w_in[l]).astype(BF16)).reshape(bsz, s, PROJ_PAD)
        gate_b = zpad(jnp.concatenate([ml_i_b[l], ml_f_b[l]]), SM_MI)
        ml_out = _mlstm(proj, ml_conv_w[l], ml_conv_b[l][None, :], gate_b, ml_norm_g[l][None, :], ml_chunk)
        q_r, k_r, v_t, iq_r, ik_r, iw_t = _dsa_prep(proj, cos, sinp, sinm,
                                                    zpad(idx_k_norm_g[l], 0), zpad(idx_k_norm_b[l], 0))
        at_out = _dsa_attention(q_r, k_r, v_t, iq_r, ik_r, iw_t)
        wo = w_out[l].astype(BF16)
        x2d = _out_proj_ln(ml_out.reshape(n, ML_WIDTH), at_out.reshape(n, AT_WIDTH), x2d,
                           wo[:ML_WIDTH], wo[ML_WIDTH:], ln1_g[l][None, :], ln1_b[l][None, :], alpha)
        j = l // 2
        if l % 2 == 0:
            x2d = _dense_ffn_ln(x2d, ffn_w_gate[j].astype(BF16), ffn_w_up[j].astype(BF16),
                                ffn_w_down[j].astype(BF16), ln2_g[l][None, :], ln2_b[l][None, :], alpha)
        else:
            wr_t = moe_w_router[j].T
            wr_hi = wr_t.astype(BF16)
            wr_lo = (wr_t - wr_hi.astype(F32)).astype(BF16)
            br = jnp.broadcast_to(moe_b_router[j][:, None], (N_EXPERTS, LANES))
            comb_t, dest_t, cnt = _router(x2d, wr_hi, wr_lo, br, moe_tb)
            counts = cnt[:, :, :moe_tb // MOE_TOKEN_CHUNK + 1].reshape(-1)
            x2d = _moe_ffn_ln(x2d, counts, dest_t, comb_t, moe_w_gate[j].astype(BF16), moe_w_up[j].astype(BF16),
                              moe_w_down[j].astype(BF16), ln2_g[l][None, :], ln2_b[l][None, :], alpha, moe_tb)
    return x2d.reshape(bsz, s, d)
```

```python
import functools

import jax
import jax.numpy as jnp
import numpy as np
from jax import lax
from jax.experimental import pallas as pl
from jax.experimental.pallas import tpu as pltpu

F32 = jnp.float32
BF16 = jnp.bfloat16
I32 = jnp.int32

ML_HEADS = 4
ML_DIM = 128
ML_WIDTH = ML_HEADS * ML_DIM
CONV_W = 4
AT_HEADS = 8
AT_KV_HEADS = 2
AT_DIM = 64
AT_WIDTH = AT_HEADS * AT_DIM
KV_WIDTH = AT_KV_HEADS * AT_DIM
IDX_HEADS = 4
IDX_DIM = 64
TOPK_MAX = 256
ROPE_THETA = 500000.0
N_EXPERTS = 8
LN_EPS = 1e-5

LANES = 128
SUBLANES = 8
VMEM_LIMIT_BYTES = 56 * 1024 * 1024

COL_MQ = 0
COL_MK = 512
COL_MV = 1024
COL_MO = 1536
COL_AQ = 2048
COL_AKV = 2560
COL_IQ = 2816
COL_SMALL = 3072
PROJ_PAD = 3200
SM_IW = 64
SM_MI = 68
SM_MF = 72

MOE_TOKEN_CHUNK = 2 * LANES
MOE_ROW_TILE = LANES

NEG_BIG = -1e30
BF16_INF_PATTERN = 0x7F80
NT_DIMS = (((1,), (1,)), ((), ()))
TN_DIMS = (((0,), (0,)), ((), ()))


def _cparams(n_axes):
    return pltpu.CompilerParams(dimension_semantics=("arbitrary",) * n_axes,
                                vmem_limit_bytes=VMEM_LIMIT_BYTES)


def _dot(a, b):
    return jnp.dot(a, b, preferred_element_type=F32)


def _layer_norm(z, g, b):
    mu = jnp.mean(z, axis=-1, keepdims=True)
    d = z - mu
    var = jnp.mean(d * d, axis=-1, keepdims=True)
    return d * lax.rsqrt(var + LN_EPS) * g + b


def _silu(x):
    return x / (1.0 + jnp.exp(-x))


def _in_proj_kernel(x_ref, w_ref, o_ref):
    o_ref[...] = _dot(x_ref[...].astype(BF16), w_ref[...])


def _in_proj(x2d, w_bf):
    n, d = x2d.shape
    pw = w_bf.shape[1]
    tm = 512
    return pl.pallas_call(
        _in_proj_kernel,
        grid=(n // tm,),
        in_specs=[pl.BlockSpec((tm, d), lambda i: (i, 0)),
                  pl.BlockSpec((d, pw), lambda i: (0, 0))],
        out_specs=pl.BlockSpec((tm, pw), lambda i: (i, 0)),
        out_shape=jax.ShapeDtypeStruct((n, pw), F32),
        compiler_params=_cparams(1),
        name="in_proj",
    )(x2d, w_bf)


def _mlstm_kernel(q_ref, k_ref, v_ref, o_ref, sm_ref, cw_ref, cb_ref, gb_ref, ng_ref, out_ref,
                  c_st, n_st, m_st, tail):
    chunk = q_ref.shape[0]
    ci = pl.program_id(1)

    @pl.when(ci == 0)
    def _():
        c_st[...] = jnp.zeros_like(c_st)
        n_st[...] = jnp.zeros_like(n_st)
        m_st[...] = jnp.zeros_like(m_st)
        tail[...] = jnp.zeros_like(tail)

    def conv_silu(x, prev, w, b):
        cat = jnp.concatenate([prev, x], axis=0)
        y = pltpu.roll(cat, 3, 0)[SUBLANES:] * w[0:1]
        y = y + pltpu.roll(cat, 2, 0)[SUBLANES:] * w[1:2]
        y = y + pltpu.roll(cat, 1, 0)[SUBLANES:] * w[2:3]
        y = y + x * w[3:4]
        return _silu(y + b)

    xq = q_ref[...]
    xk = k_ref[...]
    cw = cw_ref[...]
    cb = cb_ref[...]
    qs = conv_silu(xq, tail[:, :ML_WIDTH], cw[:, :ML_WIDTH], cb[:, :ML_WIDTH]) * (ML_DIM ** -0.5)
    ks = conv_silu(xk, tail[:, ML_WIDTH:], cw[:, ML_WIDTH:], cb[:, ML_WIDTH:])
    tail[:, :ML_WIDTH] = xq[chunk - SUBLANES:]
    tail[:, ML_WIDTH:] = xk[chunk - SUBLANES:]

    gates = sm_ref[...] + gb_ref[...]
    logf = jnp.minimum(gates, 0.0) - jnp.log1p(jnp.exp(-jnp.abs(gates)))
    row = lax.broadcasted_iota(I32, (chunk, chunk), 0)
    col = lax.broadcasted_iota(I32, (chunk, chunk), 1)
    causal = row >= col
    tri = jnp.where(causal, 1.0, 0.0).astype(BF16)
    hi = logf.astype(BF16)
    r1 = logf - hi.astype(F32)
    mid = r1.astype(BF16)
    lo = (r1 - mid.astype(F32)).astype(BF16)
    bcum = _dot(tri, hi) + _dot(tri, mid) + _dot(tri, lo)
    gates_t = gates.T
    bcum_t = bcum.T

    for h in range(ML_HEADS):
        sl = slice(h * ML_DIM, (h + 1) * ML_DIM)
        qh = qs[:, sl]
        kh = ks[:, sl]
        vb = v_ref[:, sl].astype(BF16)
        b = bcum[:, SM_MF + h:SM_MF + h + 1]
        li = gates[:, SM_MI + h:SM_MI + h + 1]
        r = gates_t[SM_MI + h:SM_MI + h + 1, :] - bcum_t[SM_MF + h:SM_MF + h + 1, :]
        m_prev = m_st[h][0:1, 0:1]
        dmat = jnp.where(causal, b + r, -jnp.inf)
        g_inter = b + m_prev
        m_t = jnp.maximum(g_inter, jnp.max(dmat, axis=1, keepdims=True))
        w_inter = jnp.exp(g_inter - m_t)
        qb = qh.astype(BF16)
        kb = kh.astype(BF16)
        sc = lax.dot_general(qb, kb, NT_DIMS, preferred_element_type=F32) * jnp.exp(dmat - m_t)
        c_old = c_st[h]
        n_old = n_st[h][0:1, :]
        num = w_inter * _dot(qb, c_old.astype(BF16)) + _dot(sc.astype(BF16), vb)
        den = w_inter * jnp.sum(qh * n_old, axis=1, keepdims=True) + jnp.sum(sc, axis=1, keepdims=True)
        hh = num / jnp.maximum(jnp.abs(den), jnp.exp(-m_t))
        b_last = b[chunk - 1:chunk, :]
        g_state = b_last + m_prev
        ls = b_last - b + li
        m_new = jnp.maximum(g_state, jnp.max(ls, axis=0, keepdims=True))
        ws = jnp.exp(ls - m_new)
        decay = jnp.exp(g_state - m_new)
        kw = kh * ws
        c_st[h] = decay * c_old + _dot(kw.T.astype(BF16), vb)
        n_st[h] = jnp.broadcast_to(decay * n_old + jnp.sum(kw, axis=0, keepdims=True), (SUBLANES, ML_DIM))
        m_st[h] = jnp.broadcast_to(m_new, (SUBLANES, LANES))
        mu = jnp.mean(hh, axis=1, keepdims=True)
        d = hh - mu
        var = jnp.mean(d * d, axis=1, keepdims=True)
        hn = d * lax.rsqrt(var + LN_EPS) * ng_ref[:, sl]
        out_ref[:, sl] = (hn / (1.0 + jnp.exp(-o_ref[:, sl]))).astype(out_ref.dtype)


def _mlstm(proj, conv_w, conv_b, gate_b, norm_g, chunk):
    bsz, s, _ = proj.shape
    wblk = ML_WIDTH

    def colspec(col, width):
        return pl.BlockSpec((None, chunk, width), lambda b, c: (b, c, col // width))

    def full2d(shape):
        return pl.BlockSpec(shape, lambda b, c: (0, 0))

    return pl.pallas_call(
        _mlstm_kernel,
        grid=(bsz, s // chunk),
        in_specs=[colspec(COL_MQ, wblk), colspec(COL_MK, wblk), colspec(COL_MV, wblk), colspec(COL_MO, wblk),
                  colspec(COL_SMALL, LANES),
                  full2d((CONV_W, 2 * ML_WIDTH)), full2d((1, 2 * ML_WIDTH)), full2d((1, LANES)),
                  full2d((1, ML_WIDTH))],
        out_specs=pl.BlockSpec((None, chunk, ML_WIDTH), lambda b, c: (b, c, 0)),
        out_shape=jax.ShapeDtypeStruct((bsz, s, ML_WIDTH), BF16),
        scratch_shapes=[pltpu.VMEM((ML_HEADS, ML_DIM, ML_DIM), F32),
                        pltpu.VMEM((ML_HEADS, SUBLANES, ML_DIM), F32),
                        pltpu.VMEM((ML_HEADS, SUBLANES, LANES), F32),
                        pltpu.VMEM((SUBLANES, 2 * ML_WIDTH), F32)],
        compiler_params=_cparams(2),
        name="mlstm",
    )(proj, proj, proj, proj, proj, conv_w, conv_b, gate_b, norm_g)


def _rope(x, cos, sinp, sinm):
    width = x.shape[1]
    reps = width // LANES
    if reps > 1:
        cos = jnp.concatenate([cos] * reps, axis=1)
        sinp = jnp.concatenate([sinp] * reps, axis=1)
        sinm = jnp.concatenate([sinm] * reps, axis=1)
    half = AT_DIM // 8
    return x * cos + pltpu.roll(x, half, 1) * sinp + pltpu.roll(x, width - half, 1) * sinm


def _dsa_prep_kernel(aq_ref, akv_ref, iq_ref, sm_ref, cos_ref, sinp_ref, sinm_ref, lng_ref, lnb_ref,
                     qt_out, k_out, vt_out, iqt_out, ik_out, iwt_out):
    cos = cos_ref[...]
    sinp = sinp_ref[...]
    sinm = sinm_ref[...]
    q = _rope(aq_ref[...], cos, sinp, sinm) * (AT_DIM ** -0.5)
    sm = sm_ref[...]
    lane = lax.broadcasted_iota(I32, sm.shape, 1)
    rep = AT_HEADS // AT_KV_HEADS
    for pair in range(AT_HEADS // 2):
        slab = q[:, pair * LANES:(pair + 1) * LANES]
        swapped = pltpu.roll(slab, AT_DIM, 1)
        for p in range(2):
            h = 2 * pair + p
            g = h // rep
            src = slab if p == g else swapped
            in_group = (lane >= g * AT_DIM) & (lane < (g + 1) * AT_DIM)
            qt_out[h * LANES:(h + 1) * LANES, :] = jnp.where(in_group, src, 0.0).T.astype(BF16)
    akv = akv_ref[...]
    k_out[...] = _rope(akv[:, :KV_WIDTH], cos, sinp, sinm).astype(BF16)
    vt_out[...] = akv[:, KV_WIDTH:].T.astype(BF16)
    iqt_out[...] = (_rope(iq_ref[...], cos, sinp, sinm) * (IDX_DIM ** -0.5)).T.astype(BF16)
    is_k = lane < IDX_DIM
    mu = jnp.sum(jnp.where(is_k, sm, 0.0), axis=1, keepdims=True) * (1.0 / IDX_DIM)
    d = jnp.where(is_k, sm - mu, 0.0)
    var = jnp.sum(d * d, axis=1, keepdims=True) * (1.0 / IDX_DIM)
    y = jnp.where(is_k, d * lax.rsqrt(var + LN_EPS) * lng_ref[...] + lnb_ref[...], 0.0)
    ik_out[...] = _rope(y, cos, sinp, sinm).astype(BF16)
    iwt_out[...] = sm.T[SM_IW:SM_IW + SUBLANES, :] * (IDX_HEADS ** -0.5)


def _dsa_prep(proj, cos, sinp, sinm, ln_g, ln_b):
    bsz, s, _ = proj.shape
    t = 512

    def colspec(col, width):
        return pl.BlockSpec((None, t, width), lambda b, c: (b, c, col // width))

    tab = pl.BlockSpec((t, LANES), lambda b, c: (c, 0))
    vec = pl.BlockSpec((1, LANES), lambda b, c: (0, 0))

    def outspec(width):
        return pl.BlockSpec((None, t, width), lambda b, c: (b, c, 0))

    return pl.pallas_call(
        _dsa_prep_kernel,
        grid=(bsz, s // t),
        in_specs=[colspec(COL_AQ, AT_WIDTH), colspec(COL_AKV, 2 * KV_WIDTH), colspec(COL_IQ, IDX_HEADS * IDX_DIM),
                  colspec(COL_SMALL, LANES), tab, tab, tab, vec, vec],
        out_specs=[pl.BlockSpec((None, AT_HEADS * LANES, t), lambda b, c: (b, 0, c)),
                   outspec(KV_WIDTH),
                   pl.BlockSpec((None, KV_WIDTH, t), lambda b, c: (b, 0, c)),
                   pl.BlockSpec((None, IDX_HEADS * IDX_DIM, t), lambda b, c: (b, 0, c)),
                   outspec(LANES),
                   pl.BlockSpec((None, SUBLANES, t), lambda b, c: (b, 0, c))],
        out_shape=[jax.ShapeDtypeStruct((bsz, AT_HEADS * LANES, s), BF16),
                   jax.ShapeDtypeStruct((bsz, s, KV_WIDTH), BF16),
                   jax.ShapeDtypeStruct((bsz, KV_WIDTH, s), BF16),
                   jax.ShapeDtypeStruct((bsz, IDX_HEADS * IDX_DIM, s), BF16),
                   jax.ShapeDtypeStruct((bsz, s, LANES), BF16),
                   jax.ShapeDtypeStruct((bsz, SUBLANES, s), F32)],
        compiler_params=_cparams(2),
        name="dsa_prep",
    )(proj, proj, proj, proj, cos, sinp, sinm, ln_g, ln_b)


def _tree_sum(parts):
    while len(parts) > 1:
        parts = [parts[i] + parts[i + 1] for i in range(0, len(parts) - 1, 2)] + ([parts[-1]] if len(parts) % 2 else [])
    return parts[0]


def _dsa_kernel(qt_ref, iqt_ref, iwt_ref, k_ref, vt_ref, ik_ref, out_ref, score, coarse, bias, acc, m_s, l_s, x_s,
                *, topk, seq, ck):
    tq = qt_ref.shape[1]
    j = pl.program_id(1)
    nck = (j * tq + tq + ck - 1) // ck
    t_abs = j * tq + lax.broadcasted_iota(I32, (ck, tq), 1)
    s_loc = lax.broadcasted_iota(I32, (ck, tq), 0)
    rep = AT_HEADS // AT_KV_HEADS
    pack = 2 * SUBLANES

    iqt = iqt_ref[...]
    iwt = iwt_ref[...]
    iq_all = jnp.concatenate([iqt[h * IDX_DIM:(h + 1) * IDX_DIM, :] for h in range(IDX_HEADS)], axis=1)
    iw_all = jnp.concatenate([iwt[h:h + 1, :] for h in range(IDX_HEADS)], axis=1)

    def chunk_off(c):
        return pl.multiple_of(c * ck, ck)

    def score_body(c, carry):
        off = chunk_off(c)
        kic = ik_ref[pl.ds(off, ck), :][:, :IDX_DIM]
        sw = jnp.maximum(_dot(kic, iq_all), 0.0) * iw_all
        s = sw[:, :tq]
        for h in range(1, IDX_HEADS):
            s = s + sw[:, h * tq:(h + 1) * tq]
        s = jnp.where(off + s_loc <= t_abs, s, -jnp.inf)
        s = jnp.where(s == 0.0, 0.0, s)
        score[pl.ds(off, ck), :] = s
        coarse[pl.ds(off, ck), :] = s.astype(BF16)
        return carry

    lax.fori_loop(0, nck, score_body, 0)

    def count(pred):
        def body(c, a):
            off = chunk_off(c)
            m = jnp.where(pred(score[pl.ds(off, ck), :], off), 1, 0)
            return a + jnp.sum(m.reshape(ck // SUBLANES, SUBLANES, tq), axis=0)
        a = lax.fori_loop(0, nck, body, jnp.zeros((SUBLANES, tq), I32))
        return jnp.sum(a, axis=0, keepdims=True)

    def count_coarse(pred):
        one = jnp.ones((), BF16)
        zero = jnp.zeros((), BF16)

        def body(c, a):
            m = jnp.where(pred(coarse[pl.ds(chunk_off(c), ck), :]), one, zero).reshape(ck // pack, pack, tq)
            return a + _tree_sum([m[i] for i in range(ck // pack)]).astype(F32)
        a = lax.fori_loop(0, nck, body, jnp.zeros((pack, tq), F32))
        return jnp.sum(a, axis=0, keepdims=True).astype(I32)

    def key_to_f32(k):
        return pltpu.bitcast(k ^ ((k >> 31) & jnp.int32(0x7FFFFFFF)), F32)

    def key16_to_bf16(k):
        pattern = (k ^ ((k >> 15) & 0x7FFF)) & 0xFFFF
        return pltpu.bitcast(pattern << 16, F32).astype(BF16)

    lo16 = -(1 << 15)
    t16 = jnp.full((1, tq), lo16, I32)
    for bit in reversed(range(16)):
        cand = t16 + (1 << bit)
        cand_b = key16_to_bf16(cand)
        c = count_coarse(lambda hc, cand_b=cand_b: hc >= cand_b)
        t16 = jnp.where(c >= topk, cand, t16)
    key_lo = jnp.clip(t16 - 1, lo16, BF16_INF_PATTERN - 1) << 16
    offs = jnp.zeros((1, tq), I32)
    for bit in reversed(range(18)):
        cand = offs + (1 << bit)
        cand_f = key_to_f32(key_lo + cand)
        c = count(lambda sc, off, cand_f=cand_f: sc >= cand_f)
        offs = jnp.where(c >= topk, cand, offs)
    thr = key_to_f32(key_lo + offs)

    take_all = t_abs[0:1, :] < topk
    n_gt = count(lambda sc, off: sc > thr)
    n_eq = count(lambda sc, off: sc == thr)
    need = topk - n_gt
    excess = (n_eq > need) & jnp.logical_not(take_all)
    x_s[...] = jnp.full(x_s.shape, seq, I32)

    any_excess = jnp.max(jnp.where(excess, 1, 0)) > 0

    @pl.when(any_excess)
    def _():
        x = jnp.zeros((1, tq), I32)
        for bit in reversed(range(max(seq - 1, 1).bit_length())):
            cand = x + (1 << bit)
            c = count(lambda sc, off: (sc == thr) & (off + s_loc < cand))
            x = jnp.where(c < need, cand, x)
        x_s[...] = jnp.broadcast_to(jnp.where(excess, x, seq), x_s.shape)

    def write_bias(with_ties):
        xlim = x_s[0:1, :]

        def bias_body(c, carry):
            off = chunk_off(c)
            sc = score[pl.ds(off, ck), :]
            s_abs = off + s_loc
            if with_ties:
                sel = (sc > thr) | ((sc == thr) & (s_abs <= xlim)) | take_all
            else:
                sel = (sc >= thr) | take_all
            bias[pl.ds(off, ck), :] = jnp.where(sel & (s_abs <= t_abs), 0.0, NEG_BIG)
            return carry

        lax.fori_loop(0, nck, bias_body, 0)

    pl.when(any_excess)(functools.partial(write_bias, True))
    pl.when(jnp.logical_not(any_excess))(functools.partial(write_bias, False))

    m_s[...] = jnp.full(m_s.shape, NEG_BIG, F32)
    l_s[...] = jnp.zeros_like(l_s)
    acc[...] = jnp.zeros_like(acc)

    def att_body(c, carry):
        off = chunk_off(c)
        kc = k_ref[pl.ds(off, ck), :]
        vtc = vt_ref[:, pl.ds(off, ck)]
        bc = bias[pl.ds(off, ck), :]
        b_all = jnp.concatenate([bc] * rep, axis=1)
        for g in range(AT_KV_HEADS):
            qg = jnp.concatenate([qt_ref[(g * rep + r) * LANES:(g * rep + r + 1) * LANES, :] for r in range(rep)],
                                 axis=1)
            lg = _dot(kc, qg) + b_all
            m_old = m_s[g]
            m_new = jnp.maximum(m_old, jnp.max(lg, axis=0, keepdims=True))
            a = jnp.exp(m_old - m_new)
            p = jnp.exp(lg - m_new)
            l_s[g] = a * l_s[g] + jnp.sum(p, axis=0, keepdims=True)
            acc[g] = a * acc[g] + _dot(vtc, p.astype(BF16))
            m_s[g] = m_new
        return carry

    lax.fori_loop(0, nck, att_body, 0)

    lane = lax.broadcasted_iota(I32, (tq, LANES), 1)
    for g in range(AT_KV_HEADS):
        on = acc[g] / l_s[g]
        for pp in range(rep // 2):
            t0 = on[:, 2 * pp * tq:(2 * pp + 1) * tq].T
            t1 = on[:, (2 * pp + 1) * tq:(2 * pp + 2) * tq].T
            if g == 0:
                slab = jnp.where(lane < AT_DIM, t0, pltpu.roll(t1, AT_DIM, 1))
            else:
                slab = jnp.where(lane < AT_DIM, pltpu.roll(t0, AT_DIM, 1), t1)
            pair = g * (rep // 2) + pp
            out_ref[:, pair * LANES:(pair + 1) * LANES] = slab.astype(out_ref.dtype)


def _dsa_attention(q_t, k_r, v_t, iq_t, ik_r, iw_t):
    bsz, s, _ = k_r.shape
    tq = min(512, s)
    topk = min(TOPK_MAX, s // 4)
    rep = AT_HEADS // AT_KV_HEADS
    ck = min(512, s)
    assert s % ck == 0 and ck >= topk and ck % tq == 0
    kern = functools.partial(_dsa_kernel, topk=topk, seq=s, ck=ck)
    return pl.pallas_call(
        kern,
        grid=(bsz, s // tq),
        in_specs=[pl.BlockSpec((None, AT_HEADS * LANES, tq), lambda b, j: (b, 0, j)),
                  pl.BlockSpec((None, IDX_HEADS * IDX_DIM, tq), lambda b, j: (b, 0, j)),
                  pl.BlockSpec((None, SUBLANES, tq), lambda b, j: (b, 0, j)),
                  pl.BlockSpec((None, s, KV_WIDTH), lambda b, j: (b, 0, 0)),
                  pl.BlockSpec((None, KV_WIDTH, s), lambda b, j: (b, 0, 0)),
                  pl.BlockSpec((None, s, LANES), lambda b, j: (b, 0, 0))],
        out_specs=pl.BlockSpec((None, tq, AT_WIDTH), lambda b, j: (b, j, 0)),
        out_shape=jax.ShapeDtypeStruct((bsz, s, AT_WIDTH), BF16),
        scratch_shapes=[pltpu.VMEM((s, tq), F32),
                        pltpu.VMEM((s, tq), BF16),
                        pltpu.VMEM((s, tq), F32),
                        pltpu.VMEM((AT_KV_HEADS, LANES, rep * tq), F32),
                        pltpu.VMEM((AT_KV_HEADS, 1, rep * tq), F32),
                        pltpu.VMEM((AT_KV_HEADS, 1, rep * tq), F32),
                        pltpu.VMEM((SUBLANES, tq), I32)],
        compiler_params=_cparams(2),
        name="dsa_attn",
    )(q_t, iq_t, iw_t, k_r, v_t, ik_r)


def _out_ln_kernel(ml_ref, at_ref, x_ref, wt_ref, wb_ref, g_ref, b_ref, o_ref, *, alpha):
    mix = _dot(ml_ref[...], wt_ref[...]) + _dot(at_ref[...], wb_ref[...])
    o_ref[...] = _layer_norm(alpha * x_ref[...] + mix, g_ref[...], b_ref[...])


def _out_proj_ln(ml, at, x2d, w_top, w_bot, g, b, alpha):
    n, d = x2d.shape
    tm = 512
    row = lambda i: (i, 0)
    const = lambda i: (0, 0)
    return pl.pallas_call(
        functools.partial(_out_ln_kernel, alpha=alpha),
        grid=(n // tm,),
        in_specs=[pl.BlockSpec((tm, ML_WIDTH), row), pl.BlockSpec((tm, AT_WIDTH), row), pl.BlockSpec((tm, d), row),
                  pl.BlockSpec((ML_WIDTH, d), const), pl.BlockSpec((AT_WIDTH, d), const),
                  pl.BlockSpec((1, d), const), pl.BlockSpec((1, d), const)],
        out_specs=pl.BlockSpec((tm, d), row),
        out_shape=jax.ShapeDtypeStruct((n, d), F32),
        compiler_params=_cparams(1),
        name="out_proj_ln",
    )(ml, at, x2d, w_top, w_bot, g, b)


def _ffn_kernel(x_ref, wg_ref, wu_ref, wd_ref, g_ref, b_ref, o_ref, acc_ref, xb_ref, *, alpha):
    f = pl.program_id(1)

    @pl.when(f == 0)
    def _():
        xb_ref[...] = x_ref[...].astype(BF16)
        acc_ref[...] = jnp.zeros_like(acc_ref)

    xb = xb_ref[...]
    h = _silu(_dot(xb, wg_ref[...])) * _dot(xb, wu_ref[...])
    acc_ref[...] += _dot(h.astype(BF16), wd_ref[...])

    @pl.when(f == pl.num_programs(1) - 1)
    def _():
        o_ref[...] = _layer_norm(alpha * x_ref[...] + acc_ref[...], g_ref[...], b_ref[...])


def _ffn_chunk(d_ff):
    best = LANES
    for c in range(LANES, 1408 + 1, LANES):
        if d_ff % c == 0:
            best = c
    return best


def _dense_ffn_ln(x2d, wg, wu, wd, g, b, alpha):
    n, d = x2d.shape
    d_ff = wg.shape[1]
    tm = 512
    fc = _ffn_chunk(d_ff)
    return pl.pallas_call(
        functools.partial(_ffn_kernel, alpha=alpha),
        grid=(n // tm, d_ff // fc),
        in_specs=[pl.BlockSpec((tm, d), lambda i, f: (i, 0)),
                  pl.BlockSpec((d, fc), lambda i, f: (0, f)),
                  pl.BlockSpec((d, fc), lambda i, f: (0, f)),
                  pl.BlockSpec((fc, d), lambda i, f: (f, 0)),
                  pl.BlockSpec((1, d), lambda i, f: (0, 0)),
                  pl.BlockSpec((1, d), lambda i, f: (0, 0))],
        out_specs=pl.BlockSpec((tm, d), lambda i, f: (i, 0)),
        out_shape=jax.ShapeDtypeStruct((n, d), F32),
        scratch_shapes=[pltpu.VMEM((tm, d), F32), pltpu.VMEM((tm, d), BF16)],
        compiler_params=_cparams(2),
        name="dense_ffn_ln",
    )(x2d, wg, wu, wd, g, b)


def _router_kernel(x_ref, wh_ref, wl_ref, br_ref, comb_ref, dest_ref, cnt_ref):
    tb = x_ref.shape[0]
    x = x_ref[...]
    xh = x.astype(BF16)
    xl = (x - xh.astype(F32)).astype(BF16)
    wh = wh_ref[...]
    lg = (lax.dot_general(wh, xh, NT_DIMS, preferred_element_type=F32)
          + lax.dot_general(wh, xl, NT_DIMS, preferred_element_type=F32)
          + lax.dot_general(wl_ref[...], xh, NT_DIMS, preferred_element_type=F32)) + br_ref[:, 0:1]
    e_id = lax.broadcasted_iota(I32, (N_EXPERTS, tb), 0)
    m1 = jnp.max(lg, axis=0, keepdims=True)
    i1 = jnp.min(jnp.where(lg == m1, e_id, N_EXPERTS), axis=0, keepdims=True)
    lg2 = jnp.where(e_id == i1, -jnp.inf, lg)
    m2 = jnp.max(lg2, axis=0, keepdims=True)
    i2 = jnp.min(jnp.where(lg2 == m2, e_id, N_EXPERTS), axis=0, keepdims=True)
    ex = jnp.exp(m2 - m1)
    g1 = 1.0 / (1.0 + ex)
    g2 = ex / (1.0 + ex)
    comb = jnp.where(e_id == i1, g1, 0.0) + jnp.where(e_id == i2, g2, 0.0)
    comb_ref[...] = comb
    mask = jnp.where((e_id == i1) | (e_id == i2), 1.0, 0.0)
    w = MOE_TOKEN_CHUNK
    upper = jnp.where(lax.broadcasted_iota(I32, (w, w), 0) < lax.broadcasted_iota(I32, (w, w), 1), 1.0, 0.0).astype(BF16)
    lane = lax.broadcasted_iota(I32, (N_EXPERTS, LANES), 1)
    cnt = jnp.zeros((N_EXPERTS, LANES), F32)
    off = jnp.zeros((N_EXPERTS, 1), F32)
    for c in range(tb // w):
        cnt = jnp.where(lane == c, off, cnt)
        mc = mask[:, c * w:(c + 1) * w]
        dest_ref[:, c * w:(c + 1) * w] = (_dot(mc.astype(BF16), upper) + off).astype(I32)
        off = off + jnp.sum(mc, axis=1, keepdims=True)
    cnt_ref[...] = jnp.where(lane == tb // w, off, cnt).astype(I32)


def _router(x2d, w_hi, w_lo, b_r, tb):
    n, d = x2d.shape
    nb = n // tb
    return pl.pallas_call(
        _router_kernel,
        grid=(nb,),
        in_specs=[pl.BlockSpec((tb, d), lambda i: (i, 0)),
                  pl.BlockSpec((N_EXPERTS, d), lambda i: (0, 0)),
                  pl.BlockSpec((N_EXPERTS, d), lambda i: (0, 0)),
                  pl.BlockSpec((N_EXPERTS, LANES), lambda i: (0, 0))],
        out_specs=[pl.BlockSpec((N_EXPERTS, tb), lambda i: (0, i)),
                   pl.BlockSpec((N_EXPERTS, tb), lambda i: (0, i)),
                   pl.BlockSpec((None, N_EXPERTS, LANES), lambda i: (i, 0, 0))],
        out_shape=[jax.ShapeDtypeStruct((N_EXPERTS, n), F32),
                   jax.ShapeDtypeStruct((N_EXPERTS, n), I32),
                   jax.ShapeDtypeStruct((nb, N_EXPERTS, LANES), I32)],
        compiler_params=_cparams(1),
        name="moe_router",
    )(x2d, w_hi, w_lo, b_r)


def _moe_kernel(cnt_ref, x_ref, dest_ref, comb_ref, wg_ref, wu_ref, wd_ref, g_ref, b_ref, o_ref,
                xb, xe, ye, gs, *, alpha):
    sub = MOE_ROW_TILE
    cw = MOE_TOKEN_CHUNK
    tb, d = x_ref.shape
    nch = tb // cw
    i = pl.program_id(0)
    e = pl.program_id(1)
    f = pl.program_id(2)
    n_e = pl.num_programs(1)
    n_f = pl.num_programs(2)
    base = (i * n_e + e) * (nch + 1)
    cum = [cnt_ref[base + k] for k in range(nch + 1)]
    nsub = (cum[nch] + sub - 1) // sub

    @pl.when((e == 0) & (f == 0))
    def _():
        xb[...] = x_ref[...].astype(BF16)
        o_ref[...] = jnp.zeros_like(o_ref)

    def onehot(r0, ch):
        cols = slice(ch * cw, (ch + 1) * cw)
        drow = dest_ref[pl.ds(e, 1), cols]
        crow = comb_ref[pl.ds(e, 1), cols]
        slot = r0 + lax.broadcasted_iota(I32, (sub, cw), 0)
        return jnp.where((drow == slot) & (crow > 0.0), 1.0, 0.0).astype(BF16)

    def for_overlapping_chunks(r0, fn):
        for ch in range(nch):
            pl.when((cum[ch] < r0 + sub) & (cum[ch + 1] > r0))(functools.partial(fn, ch))

    @pl.when(f == 0)
    def _():
        def body(r, carry):
            r0 = pl.multiple_of(r * sub, sub)
            rows = pl.ds(r0, sub)
            xe[rows, :] = jnp.zeros((sub, d), BF16)
            ye[rows, :] = jnp.zeros((sub, d), F32)
            gs[rows, :] = jnp.zeros((sub, LANES), F32)

            def gather(ch):
                cols = slice(ch * cw, (ch + 1) * cw)
                p = onehot(r0, ch)
                xe[rows, :] += _dot(p, xb[cols, :]).astype(BF16)
                crow = comb_ref[pl.ds(e, 1), cols]
                hi = crow.astype(BF16)
                lo = (crow - hi.astype(F32)).astype(BF16)
                c2 = jnp.concatenate([hi, lo, jnp.zeros((LANES - 2, cw), BF16)], axis=0)
                gs[rows, :] += lax.dot_general(p, c2, NT_DIMS, preferred_element_type=F32)

            for_overlapping_chunks(r0, gather)
            return carry
        lax.fori_loop(0, nsub, body, 0)

    def ffn_tile(r0, m):
        rows = pl.ds(r0, m)
        xr = xe[rows, :]
        h = _silu(_dot(xr, wg_ref[...])) * _dot(xr, wu_ref[...])
        ye[rows, :] += _dot(h.astype(BF16), wd_ref[...])

    def ffn_pair(r, carry):
        ffn_tile(pl.multiple_of(r * 2 * sub, 2 * sub), 2 * sub)
        return carry

    n_pair = nsub // 2
    lax.fori_loop(0, n_pair, ffn_pair, 0)

    @pl.when(nsub % 2 == 1)
    def _():
        ffn_tile(pl.multiple_of(n_pair * 2 * sub, 2 * sub), sub)

    @pl.when(f == n_f - 1)
    def _():
        def body(r, carry):
            r0 = pl.multiple_of(r * sub, sub)
            rows = pl.ds(r0, sub)
            gate = gs[rows, 0:1] + gs[rows, 1:2]
            yw = (ye[rows, :] * gate).astype(BF16)

            def scatter(ch):
                cols = slice(ch * cw, (ch + 1) * cw)
                o_ref[cols, :] += lax.dot_general(onehot(r0, ch), yw, TN_DIMS, preferred_element_type=F32)

            for_overlapping_chunks(r0, scatter)
            return carry
        lax.fori_loop(0, nsub, body, 0)

    @pl.when((e == n_e - 1) & (f == n_f - 1))
    def _():
        o_ref[...] = _layer_norm(alpha * x_ref[...] + o_ref[...], g_ref[...], b_ref[...])


def _moe_ffn_ln(x2d, counts, dest_t, comb_t, wg, wu, wd, g, b, alpha, tb):
    n, d = x2d.shape
    n_e, _, d_ff = wg.shape
    fc = 512
    grid_spec = pltpu.PrefetchScalarGridSpec(
        num_scalar_prefetch=1,
        grid=(n // tb, n_e, d_ff // fc),
        in_specs=[pl.BlockSpec((tb, d), lambda i, e, f, cnt: (i, 0), pipeline_mode=pl.Buffered(1)),
                  pl.BlockSpec((n_e, tb), lambda i, e, f, cnt: (0, i)),
                  pl.BlockSpec((n_e, tb), lambda i, e, f, cnt: (0, i)),
                  pl.BlockSpec((None, d, fc), lambda i, e, f, cnt: (e, 0, f)),
                  pl.BlockSpec((None, d, fc), lambda i, e, f, cnt: (e, 0, f)),
                  pl.BlockSpec((None, fc, d), lambda i, e, f, cnt: (e, f, 0)),
                  pl.BlockSpec((1, d), lambda i, e, f, cnt: (0, 0)),
                  pl.BlockSpec((1, d), lambda i, e, f, cnt: (0, 0))],
        out_specs=pl.BlockSpec((tb, d), lambda i, e, f, cnt: (i, 0)),
        scratch_shapes=[pltpu.VMEM((tb, d), BF16), pltpu.VMEM((tb, d), BF16), pltpu.VMEM((tb, d), F32),
                        pltpu.VMEM((tb, LANES), F32)],
    )
    return pl.pallas_call(
        functools.partial(_moe_kernel, alpha=alpha),
        grid_spec=grid_spec,
        out_shape=jax.ShapeDtypeStruct((n, d), F32),
        compiler_params=_cparams(3),
        name="moe_ffn_ln",
    )(counts, x2d, dest_t, comb_t, wg, wu, wd, g, b)


def _rope_tables(s):
    half = AT_DIM // 8
    inv = ROPE_THETA ** (-jnp.arange(half, dtype=F32) / half)
    ang = jnp.arange(s).astype(F32)[:, None] * inv[None, :]
    cos = jnp.cos(ang)
    sin = jnp.sin(ang)
    ones = jnp.ones((s, AT_DIM - 2 * half), F32)
    zeros = jnp.zeros((s, AT_DIM - 2 * half), F32)
    z8 = jnp.zeros((s, half), F32)
    cos64 = jnp.concatenate([cos, cos, ones], axis=1)
    sinp64 = jnp.concatenate([z8, sin, zeros], axis=1)
    sinm64 = jnp.concatenate([-sin, z8, zeros], axis=1)
    rep = LANES // AT_DIM
    return (jnp.tile(cos64, (1, rep)), jnp.tile(sinp64, (1, rep)), jnp.tile(sinm64, (1, rep)))


def _permute_w_in(w):
    n_in = w.shape[1]
    n_ml = 4 * ML_WIDTH
    n_gate = 2 * ML_HEADS
    src = np.full((PROJ_PAD,), -1, np.int32)
    src[:n_ml] = np.arange(n_ml)
    src[n_ml:n_in - n_gate] = np.arange(n_ml + n_gate, n_in)
    src[n_in - n_gate:n_in] = np.arange(n_ml, n_ml + n_gate)
    sel = (jnp.arange(n_in, dtype=I32)[:, None] == jnp.asarray(src)[None, :]).astype(BF16)
    return jnp.dot(w.astype(BF16), sel, preferred_element_type=F32).astype(BF16)


def kernel(x, w_in, ml_conv_w, ml_conv_b, ml_i_b, ml_f_b, ml_norm_g, idx_k_norm_g, idx_k_norm_b, w_out, ln1_g, ln1_b, ln2_g, ln2_b, ffn_w_gate, ffn_w_up, ffn_w_down, moe_w_router, moe_b_router, moe_w_gate, moe_w_up, moe_w_down):
    bsz, s, d = x.shape
    depth = w_in.shape[0]
    alpha = float((2 * depth) ** 0.25)
    n = bsz * s
    cos, sinp, sinm = _rope_tables(s)
    ml_chunk = min(256, s)
    moe_tb = min(2048, n)
    zpad = lambda v, left: jnp.pad(v, (left, LANES - left - v.shape[0]))[None, :]

    x2d = x.reshape(n, d)
    for l in range(depth):
        proj = _in_proj(x2d, _permute_w_in(w_in[l]).astype(BF16)).reshape(bsz, s, PROJ_PAD)
        gate_b = zpad(jnp.concatenate([ml_i_b[l], ml_f_b[l]]), SM_MI)
        ml_out = _mlstm(proj, ml_conv_w[l], ml_conv_b[l][None, :], gate_b, ml_norm_g[l][None, :], ml_chunk)
        q_r, k_r, v_t, iq_r, ik_r, iw_t = _dsa_prep(proj, cos, sinp, sinm,
                                                    zpad(idx_k_norm_g[l], 0), zpad(idx_k_norm_b[l], 0))
        at_out = _dsa_attention(q_r, k_r, v_t, iq_r, ik_r, iw_t)
        wo = w_out[l].astype(BF16)
        x2d = _out_proj_ln(ml_out.reshape(n, ML_WIDTH), at_out.reshape(n, AT_WIDTH), x2d,
                           wo[:ML_WIDTH], wo[ML_WIDTH:], ln1_g[l][None, :], ln1_b[l][None, :], alpha)
        j = l // 2
        if l % 2 == 0:
            x2d = _dense_ffn_ln(x2d, ffn_w_gate[j].astype(BF16), ffn_w_up[j].astype(BF16),
                                ffn_w_down[j].astype(BF16), ln2_g[l][None, :], ln2_b[l][None, :], alpha)
        else:
            wr_t = moe_w_router[j].T
            wr_hi = wr_t.astype(BF16)
            wr_lo = (wr_t - wr_hi.astype(F32)).astype(BF16)
            br = jnp.broadcast_to(moe_b_router[j][:, None], (N_EXPERTS, LANES))
            comb_t, dest_t, cnt = _router(x2d, wr_hi, wr_lo, br, moe_tb)
            counts = cnt[:, :, :moe_tb // MOE_TOKEN_CHUNK + 1].reshape(-1)
            x2d = _moe_ffn_ln(x2d, counts, dest_t, comb_t, moe_w_gate[j].astype(BF16), moe_w_up[j].astype(BF16),
                              moe_w_down[j].astype(BF16), ln2_g[l][None, :], ln2_b[l][None, :], alpha, moe_tb)
    return x2d.reshape(bsz, s, d)
```

```python
import functools

import jax
import jax.numpy as jnp
import numpy as np
from jax import lax
from jax.experimental import pallas as pl
from jax.experimental.pallas import tpu as pltpu

F32 = jnp.float32
BF16 = jnp.bfloat16
I32 = jnp.int32

ML_HEADS = 4
ML_DIM = 128
ML_WIDTH = ML_HEADS * ML_DIM
CONV_W = 4
AT_HEADS = 8
AT_KV_HEADS = 2
AT_DIM = 64
AT_WIDTH = AT_HEADS * AT_DIM
KV_WIDTH = AT_KV_HEADS * AT_DIM
IDX_HEADS = 4
IDX_DIM = 64
TOPK_MAX = 256
ROPE_THETA = 500000.0
N_EXPERTS = 8
LN_EPS = 1e-5

LANES = 128
SUBLANES = 8
VMEM_LIMIT_BYTES = 56 * 1024 * 1024

COL_MQ = 0
COL_MK = 512
COL_MV = 1024
COL_MO = 1536
COL_AQ = 2048
COL_AKV = 2560
COL_IQ = 2816
COL_SMALL = 3072
PROJ_PAD = 3200
SM_IW = 64
SM_MI = 68
SM_MF = 72

MOE_TOKEN_CHUNK = 2 * LANES
MOE_ROW_TILE = LANES

NEG_BIG = -1e30
BF16_INF_PATTERN = 0x7F80
VT_ROWS = AT_DIM + 2 * SUBLANES
NT_DIMS = (((1,), (1,)), ((), ()))
TN_DIMS = (((0,), (0,)), ((), ()))


def _cparams(n_axes):
    return pltpu.CompilerParams(dimension_semantics=("arbitrary",) * n_axes,
                                vmem_limit_bytes=VMEM_LIMIT_BYTES)


def _dot(a, b):
    return jnp.dot(a, b, preferred_element_type=F32)


def _layer_norm(z, g, b):
    mu = jnp.mean(z, axis=-1, keepdims=True)
    d = z - mu
    var = jnp.mean(d * d, axis=-1, keepdims=True)
    return d * lax.rsqrt(var + LN_EPS) * g + b


def _silu(x):
    return x / (1.0 + jnp.exp(-x))


def _in_proj_kernel(x_ref, w_ref, o_ref):
    o_ref[...] = _dot(x_ref[...].astype(BF16), w_ref[...])


def _in_proj(x2d, w_bf):
    n, d = x2d.shape
    pw = w_bf.shape[1]
    tm = 512
    return pl.pallas_call(
        _in_proj_kernel,
        grid=(n // tm,),
        in_specs=[pl.BlockSpec((tm, d), lambda i: (i, 0)),
                  pl.BlockSpec((d, pw), lambda i: (0, 0))],
        out_specs=pl.BlockSpec((tm, pw), lambda i: (i, 0)),
        out_shape=jax.ShapeDtypeStruct((n, pw), F32),
        compiler_params=_cparams(1),
        name="in_proj",
    )(x2d, w_bf)


def _mlstm_kernel(q_ref, k_ref, v_ref, o_ref, sm_ref, cw_ref, cb_ref, gb_ref, ng_ref, out_ref,
                  c_st, n_st, m_st, tail):
    @pl.when(pl.program_id(1) == 0)
    def _():
        c_st[...] = jnp.zeros_like(c_st)
        n_st[...] = jnp.zeros_like(n_st)
        m_st[...] = jnp.zeros_like(m_st)
        tail[...] = jnp.zeros_like(tail)

    for bb in range(q_ref.shape[0]):
        _mlstm_chunk(q_ref.at[bb], k_ref.at[bb], v_ref.at[bb], o_ref.at[bb], sm_ref.at[bb], cw_ref, cb_ref, gb_ref,
                     ng_ref, out_ref.at[bb], c_st.at[bb], n_st.at[bb], m_st.at[bb], tail.at[bb])


def _mlstm_chunk(q_ref, k_ref, v_ref, o_ref, sm_ref, cw_ref, cb_ref, gb_ref, ng_ref, out_ref,
                 c_st, n_st, m_st, tail):
    chunk = q_ref.shape[0]

    def conv_silu(x, prev, w, b):
        cat = jnp.concatenate([prev, x], axis=0)
        y = pltpu.roll(cat, 3, 0)[SUBLANES:] * w[0:1]
        y = y + pltpu.roll(cat, 2, 0)[SUBLANES:] * w[1:2]
        y = y + pltpu.roll(cat, 1, 0)[SUBLANES:] * w[2:3]
        y = y + x * w[3:4]
        return _silu(y + b)

    xq = q_ref[...]
    xk = k_ref[...]
    cw = cw_ref[...]
    cb = cb_ref[...]
    qs = conv_silu(xq, tail[:, :ML_WIDTH], cw[:, :ML_WIDTH], cb[:, :ML_WIDTH]) * (ML_DIM ** -0.5)
    ks = conv_silu(xk, tail[:, ML_WIDTH:], cw[:, ML_WIDTH:], cb[:, ML_WIDTH:])
    tail[:, :ML_WIDTH] = xq[chunk - SUBLANES:]
    tail[:, ML_WIDTH:] = xk[chunk - SUBLANES:]

    gates = sm_ref[...] + gb_ref[...]
    logf = jnp.minimum(gates, 0.0) - jnp.log1p(jnp.exp(-jnp.abs(gates)))
    row = lax.broadcasted_iota(I32, (chunk, chunk), 0)
    col = lax.broadcasted_iota(I32, (chunk, chunk), 1)
    causal = row >= col
    tri = jnp.where(causal, 1.0, 0.0).astype(BF16)
    hi = logf.astype(BF16)
    r1 = logf - hi.astype(F32)
    mid = r1.astype(BF16)
    lo = (r1 - mid.astype(F32)).astype(BF16)
    bcum = _dot(tri, hi) + _dot(tri, mid) + _dot(tri, lo)
    gates_t = gates.T
    bcum_t = bcum.T

    for h in range(ML_HEADS):
        sl = slice(h * ML_DIM, (h + 1) * ML_DIM)
        qh = qs[:, sl]
        kh = ks[:, sl]
        vb = v_ref[:, sl].astype(BF16)
        b = bcum[:, SM_MF + h:SM_MF + h + 1]
        li = gates[:, SM_MI + h:SM_MI + h + 1]
        r = gates_t[SM_MI + h:SM_MI + h + 1, :] - bcum_t[SM_MF + h:SM_MF + h + 1, :]
        m_prev = m_st[h][0:1, 0:1]
        dmat = jnp.where(causal, b + r, -jnp.inf)
        g_inter = b + m_prev
        m_t = jnp.maximum(g_inter, jnp.max(dmat, axis=1, keepdims=True))
        w_inter = jnp.exp(g_inter - m_t)
        qb = qh.astype(BF16)
        kb = kh.astype(BF16)
        sc = lax.dot_general(qb, kb, NT_DIMS, preferred_element_type=F32) * jnp.exp(dmat - m_t)
        c_old = c_st[h]
        n_old = n_st[h][0:1, :]
        num = w_inter * _dot(qb, c_old.astype(BF16)) + _dot(sc.astype(BF16), vb)
        den = w_inter * jnp.sum(qh * n_old, axis=1, keepdims=True) + jnp.sum(sc, axis=1, keepdims=True)
        hh = num / jnp.maximum(jnp.abs(den), jnp.exp(-m_t))
        b_last = b[chunk - 1:chunk, :]
        g_state = b_last + m_prev
        ls = b_last - b + li
        m_new = jnp.maximum(g_state, jnp.max(ls, axis=0, keepdims=True))
        ws = jnp.exp(ls - m_new)
        decay = jnp.exp(g_state - m_new)
        kw = kh * ws
        c_st[h] = decay * c_old + _dot(kw.T.astype(BF16), vb)
        n_st[h] = jnp.broadcast_to(decay * n_old + jnp.sum(kw, axis=0, keepdims=True), (SUBLANES, ML_DIM))
        m_st[h] = jnp.broadcast_to(m_new, (SUBLANES, LANES))
        mu = jnp.mean(hh, axis=1, keepdims=True)
        d = hh - mu
        var = jnp.mean(d * d, axis=1, keepdims=True)
        hn = d * lax.rsqrt(var + LN_EPS) * ng_ref[:, sl]
        out_ref[:, sl] = (hn / (1.0 + jnp.exp(-o_ref[:, sl]))).astype(out_ref.dtype)


def _mlstm(proj, conv_w, conv_b, gate_b, norm_g, chunk):
    bsz, s, _ = proj.shape
    wblk = ML_WIDTH
    nb = 1

    def colspec(col, width):
        return pl.BlockSpec((nb, chunk, width), lambda b, c: (b, c, col // width))

    def full2d(shape):
        return pl.BlockSpec(shape, lambda b, c: (0, 0))

    return pl.pallas_call(
        _mlstm_kernel,
        grid=(bsz // nb, s // chunk),
        in_specs=[colspec(COL_MQ, wblk), colspec(COL_MK, wblk), colspec(COL_MV, wblk), colspec(COL_MO, wblk),
                  colspec(COL_SMALL, LANES),
                  full2d((CONV_W, 2 * ML_WIDTH)), full2d((1, 2 * ML_WIDTH)), full2d((1, LANES)),
                  full2d((1, ML_WIDTH))],
        out_specs=pl.BlockSpec((nb, chunk, ML_WIDTH), lambda b, c: (b, c, 0)),
        out_shape=jax.ShapeDtypeStruct((bsz, s, ML_WIDTH), BF16),
        scratch_shapes=[pltpu.VMEM((nb, ML_HEADS, ML_DIM, ML_DIM), F32),
                        pltpu.VMEM((nb, ML_HEADS, SUBLANES, ML_DIM), F32),
                        pltpu.VMEM((nb, ML_HEADS, SUBLANES, LANES), F32),
                        pltpu.VMEM((nb, SUBLANES, 2 * ML_WIDTH), F32)],
        compiler_params=_cparams(2),
        name="mlstm",
    )(proj, proj, proj, proj, proj, conv_w, conv_b, gate_b, norm_g)


def _rope(x, cos, sinp, sinm):
    width = x.shape[1]
    reps = width // LANES
    if reps > 1:
        cos = jnp.concatenate([cos] * reps, axis=1)
        sinp = jnp.concatenate([sinp] * reps, axis=1)
        sinm = jnp.concatenate([sinm] * reps, axis=1)
    half = AT_DIM // 8
    return x * cos + pltpu.roll(x, half, 1) * sinp + pltpu.roll(x, width - half, 1) * sinm


def _dsa_prep_kernel(aq_ref, akv_ref, iq_ref, sm_ref, cos_ref, sinp_ref, sinm_ref, lng_ref, lnb_ref,
                     qt_out, k_out, vt_out, iqt_out, ik_out, iwt_out):
    cos = cos_ref[...]
    sinp = sinp_ref[...]
    sinm = sinm_ref[...]
    q = _rope(aq_ref[...], cos, sinp, sinm) * (AT_DIM ** -0.5)
    sm = sm_ref[...]
    lane = lax.broadcasted_iota(I32, sm.shape, 1)
    rep = AT_HEADS // AT_KV_HEADS
    for pair in range(AT_HEADS // 2):
        slab = q[:, pair * LANES:(pair + 1) * LANES]
        swapped = pltpu.roll(slab, AT_DIM, 1)
        for p in range(2):
            h = 2 * pair + p
            g = h // rep
            src = slab if p == g else swapped
            in_group = (lane >= g * AT_DIM) & (lane < (g + 1) * AT_DIM)
            qt_out[h * LANES:(h + 1) * LANES, :] = jnp.where(in_group, src, 0.0).T.astype(BF16)
    akv = akv_ref[...]
    k_out[...] = _rope(akv[:, :KV_WIDTH], cos, sinp, sinm).astype(BF16)
    v_t = akv[:, KV_WIDTH:].T
    t_len = v_t.shape[1]
    tail = jnp.where(lax.broadcasted_iota(I32, (VT_ROWS - AT_DIM, t_len), 0) == 0, 1.0, 0.0)
    for g in range(AT_KV_HEADS):
        vt_out[g] = jnp.concatenate([v_t[g * AT_DIM:(g + 1) * AT_DIM], tail], axis=0).astype(BF16)
    iqt_out[...] = (_rope(iq_ref[...], cos, sinp, sinm) * (IDX_DIM ** -0.5)).T.astype(BF16)
    is_k = lane < IDX_DIM
    mu = jnp.sum(jnp.where(is_k, sm, 0.0), axis=1, keepdims=True) * (1.0 / IDX_DIM)
    d = jnp.where(is_k, sm - mu, 0.0)
    var = jnp.sum(d * d, axis=1, keepdims=True) * (1.0 / IDX_DIM)
    y = jnp.where(is_k, d * lax.rsqrt(var + LN_EPS) * lng_ref[...] + lnb_ref[...], 0.0)
    ik_out[...] = _rope(y, cos, sinp, sinm).astype(BF16)
    iwt_out[...] = sm.T[SM_IW:SM_IW + SUBLANES, :] * (IDX_HEADS ** -0.5)


def _dsa_prep(proj, cos, sinp, sinm, ln_g, ln_b):
    bsz, s, _ = proj.shape
    t = 512

    def colspec(col, width):
        return pl.BlockSpec((None, t, width), lambda b, c: (b, c, col // width))

    tab = pl.BlockSpec((t, LANES), lambda b, c: (c, 0))
    vec = pl.BlockSpec((1, LANES), lambda b, c: (0, 0))

    def outspec(width):
        return pl.BlockSpec((None, t, width), lambda b, c: (b, c, 0))

    return pl.pallas_call(
        _dsa_prep_kernel,
        grid=(bsz, s // t),
        in_specs=[colspec(COL_AQ, AT_WIDTH), colspec(COL_AKV, 2 * KV_WIDTH), colspec(COL_IQ, IDX_HEADS * IDX_DIM),
                  colspec(COL_SMALL, LANES), tab, tab, tab, vec, vec],
        out_specs=[pl.BlockSpec((None, AT_HEADS * LANES, t), lambda b, c: (b, 0, c)),
                   outspec(KV_WIDTH),
                   pl.BlockSpec((None, AT_KV_HEADS, VT_ROWS, t), lambda b, c: (b, 0, 0, c)),
                   pl.BlockSpec((None, IDX_HEADS * IDX_DIM, t), lambda b, c: (b, 0, c)),
                   outspec(LANES),
                   pl.BlockSpec((None, SUBLANES, t), lambda b, c: (b, 0, c))],
        out_shape=[jax.ShapeDtypeStruct((bsz, AT_HEADS * LANES, s), BF16),
                   jax.ShapeDtypeStruct((bsz, s, KV_WIDTH), BF16),
                   jax.ShapeDtypeStruct((bsz, AT_KV_HEADS, VT_ROWS, s), BF16),
                   jax.ShapeDtypeStruct((bsz, IDX_HEADS * IDX_DIM, s), BF16),
                   jax.ShapeDtypeStruct((bsz, s, LANES), BF16),
                   jax.ShapeDtypeStruct((bsz, SUBLANES, s), F32)],
        compiler_params=_cparams(2),
        name="dsa_prep",
    )(proj, proj, proj, proj, cos, sinp, sinm, ln_g, ln_b)


def _tree_sum(parts):
    while len(parts) > 1:
        parts = [parts[i] + parts[i + 1] for i in range(0, len(parts) - 1, 2)] + ([parts[-1]] if len(parts) % 2 else [])
    return parts[0]


def _dsa_kernel(qt_ref, iqt_ref, iwt_ref, k_ref, vt_ref, ik_ref, out_ref, score, coarse, bias, acc, m_s, x_s,
                *, topk, seq, ck):
    tq = qt_ref.shape[1]
    j = pl.program_id(1)
    nck = (j * tq + tq + ck - 1) // ck
    t_abs = j * tq + lax.broadcasted_iota(I32, (ck, tq), 1)
    s_loc = lax.broadcasted_iota(I32, (ck, tq), 0)
    rep = AT_HEADS // AT_KV_HEADS
    pack = 2 * SUBLANES

    iqt = iqt_ref[...]
    iwt = iwt_ref[...]
    iq_all = jnp.concatenate([iqt[h * IDX_DIM:(h + 1) * IDX_DIM, :] for h in range(IDX_HEADS)], axis=1)
    iw_all = jnp.concatenate([iwt[h:h + 1, :] for h in range(IDX_HEADS)], axis=1)

    def chunk_off(c):
        return pl.multiple_of(c * ck, ck)

    def score_body(c, carry):
        off = chunk_off(c)
        kic = ik_ref[pl.ds(off, ck), :][:, :IDX_DIM]
        sw = jnp.maximum(_dot(kic, iq_all), 0.0) * iw_all
        s = sw[:, :tq]
        for h in range(1, IDX_HEADS):
            s = s + sw[:, h * tq:(h + 1) * tq]
        s = jnp.where(off + s_loc <= t_abs, s, -jnp.inf)
        s = jnp.where(s == 0.0, 0.0, s)
        score[pl.ds(off, ck), :] = s
        coarse[pl.ds(off, ck), :] = s.astype(BF16)
        return carry

    lax.fori_loop(0, nck, score_body, 0)

    def count(pred):
        def body(c, a):
            off = chunk_off(c)
            m = jnp.where(pred(score[pl.ds(off, ck), :], off), 1, 0)
            return a + jnp.sum(m.reshape(ck // SUBLANES, SUBLANES, tq), axis=0)
        a = lax.fori_loop(0, nck, body, jnp.zeros((SUBLANES, tq), I32))
        return jnp.sum(a, axis=0, keepdims=True)

    def count_coarse(pred):
        one = jnp.ones((), BF16)
        zero = jnp.zeros((), BF16)

        def body(c, a):
            m = jnp.where(pred(coarse[pl.ds(chunk_off(c), ck), :]), one, zero).reshape(ck // pack, pack, tq)
            return a + _tree_sum([m[i] for i in range(ck // pack)]).astype(F32)
        a = lax.fori_loop(0, nck, body, jnp.zeros((pack, tq), F32))
        return jnp.sum(a, axis=0, keepdims=True).astype(I32)

    def key_to_f32(k):
        return pltpu.bitcast(k ^ ((k >> 31) & jnp.int32(0x7FFFFFFF)), F32)

    def key16_to_bf16(k):
        pattern = (k ^ ((k >> 15) & 0x7FFF)) & 0xFFFF
        return pltpu.bitcast(pattern << 16, F32).astype(BF16)

    lo16 = -(1 << 15)
    t16 = jnp.full((1, tq), lo16, I32)
    for bit in reversed(range(16)):
        cand = t16 + (1 << bit)
        cand_b = key16_to_bf16(cand)
        c = count_coarse(lambda hc, cand_b=cand_b: hc >= cand_b)
        t16 = jnp.where(c >= topk, cand, t16)
    key_lo = jnp.clip(t16 - 1, lo16, BF16_INF_PATTERN - 1) << 16
    offs = jnp.zeros((1, tq), I32)
    for bit in reversed(range(18)):
        cand = offs + (1 << bit)
        cand_f = key_to_f32(key_lo + cand)
        c = count(lambda sc, off, cand_f=cand_f: sc >= cand_f)
        offs = jnp.where(c >= topk, cand, offs)
    thr = key_to_f32(key_lo + offs)

    take_all = t_abs[0:1, :] < topk
    n_gt = count(lambda sc, off: sc > thr)
    n_eq = count(lambda sc, off: sc == thr)
    need = topk - n_gt
    excess = (n_eq > need) & jnp.logical_not(take_all)
    x_s[...] = jnp.full(x_s.shape, seq, I32)

    any_excess = jnp.max(jnp.where(excess, 1, 0)) > 0

    @pl.when(any_excess)
    def _():
        x = jnp.zeros((1, tq), I32)
        for bit in reversed(range(max(seq - 1, 1).bit_length())):
            cand = x + (1 << bit)
            c = count(lambda sc, off: (sc == thr) & (off + s_loc < cand))
            x = jnp.where(c < need, cand, x)
        x_s[...] = jnp.broadcast_to(jnp.where(excess, x, seq), x_s.shape)

    def write_bias(with_ties):
        xlim = x_s[0:1, :]

        def bias_body(c, carry):
            off = chunk_off(c)
            sc = score[pl.ds(off, ck), :]
            s_abs = off + s_loc
            if with_ties:
                sel = (sc > thr) | ((sc == thr) & (s_abs <= xlim)) | take_all
            else:
                sel = (sc >= thr) | take_all
            bias[pl.ds(off, ck), :] = jnp.where(sel & (s_abs <= t_abs), 0.0, NEG_BIG)
            return carry

        lax.fori_loop(0, nck, bias_body, 0)

    pl.when(any_excess)(functools.partial(write_bias, True))
    pl.when(jnp.logical_not(any_excess))(functools.partial(write_bias, False))

    m_s[...] = jnp.full(m_s.shape, NEG_BIG, F32)
    acc[...] = jnp.zeros_like(acc)

    def att_body(c, carry):
        off = chunk_off(c)
        kc = k_ref[pl.ds(off, ck), :]
        bc = bias[pl.ds(off, ck), :]
        b_all = jnp.concatenate([bc] * rep, axis=1)
        for g in range(AT_KV_HEADS):
            qg = jnp.concatenate([qt_ref[(g * rep + r) * LANES:(g * rep + r + 1) * LANES, :] for r in range(rep)],
                                 axis=1)
            lg = _dot(kc, qg) + b_all
            m_old = m_s[g]
            m_new = jnp.maximum(m_old, jnp.max(lg, axis=0, keepdims=True))
            a = jnp.exp(m_old - m_new)
            p = jnp.exp(lg - m_new)
            acc[g] = a * acc[g] + _dot(vt_ref[g, :, pl.ds(off, ck)], p.astype(BF16))
            m_s[g] = m_new
        return carry

    lax.fori_loop(0, nck, att_body, 0)

    for g in range(AT_KV_HEADS):
        ag = acc[g]
        on = ag[:AT_DIM] / ag[AT_DIM:AT_DIM + 1]
        for pp in range(rep // 2):
            two = jnp.concatenate([on[:, 2 * pp * tq:(2 * pp + 1) * tq], on[:, (2 * pp + 1) * tq:(2 * pp + 2) * tq]],
                                  axis=0)
            pair = g * (rep // 2) + pp
            out_ref[:, pair * LANES:(pair + 1) * LANES] = two.T.astype(out_ref.dtype)


def _dsa_attention(q_t, k_r, v_t, iq_t, ik_r, iw_t):
    bsz, s, _ = k_r.shape
    tq = min(512, s)
    topk = min(TOPK_MAX, s // 4)
    rep = AT_HEADS // AT_KV_HEADS
    ck = min(512, s)
    assert s % ck == 0 and ck >= topk and ck % tq == 0
    kern = functools.partial(_dsa_kernel, topk=topk, seq=s, ck=ck)
    return pl.pallas_call(
        kern,
        grid=(bsz, s // tq),
        in_specs=[pl.BlockSpec((None, AT_HEADS * LANES, tq), lambda b, j: (b, 0, j)),
                  pl.BlockSpec((None, IDX_HEADS * IDX_DIM, tq), lambda b, j: (b, 0, j)),
                  pl.BlockSpec((None, SUBLANES, tq), lambda b, j: (b, 0, j)),
                  pl.BlockSpec((None, s, KV_WIDTH), lambda b, j: (b, 0, 0)),
                  pl.BlockSpec((None, AT_KV_HEADS, VT_ROWS, s), lambda b, j: (b, 0, 0, 0)),
                  pl.BlockSpec((None, s, LANES), lambda b, j: (b, 0, 0))],
        out_specs=pl.BlockSpec((None, tq, AT_WIDTH), lambda b, j: (b, j, 0)),
        out_shape=jax.ShapeDtypeStruct((bsz, s, AT_WIDTH), BF16),
        scratch_shapes=[pltpu.VMEM((s, tq), F32),
                        pltpu.VMEM((s, tq), BF16),
                        pltpu.VMEM((s, tq), F32),
                        pltpu.VMEM((AT_KV_HEADS, VT_ROWS, rep * tq), F32),
                        pltpu.VMEM((AT_KV_HEADS, 1, rep * tq), F32),
                        pltpu.VMEM((SUBLANES, tq), I32)],
        compiler_params=_cparams(2),
        name="dsa_attn",
    )(q_t, iq_t, iw_t, k_r, v_t, ik_r)


def _out_ln_kernel(ml_ref, at_ref, x_ref, wt_ref, wb_ref, g_ref, b_ref, o_ref, *, alpha):
    mix = _dot(ml_ref[...], wt_ref[...]) + _dot(at_ref[...], wb_ref[...])
    o_ref[...] = _layer_norm(alpha * x_ref[...] + mix, g_ref[...], b_ref[...])


def _out_proj_ln(ml, at, x2d, w_top, w_bot, g, b, alpha):
    n, d = x2d.shape
    tm = 512
    row = lambda i: (i, 0)
    const = lambda i: (0, 0)
    return pl.pallas_call(
        functools.partial(_out_ln_kernel, alpha=alpha),
        grid=(n // tm,),
        in_specs=[pl.BlockSpec((tm, ML_WIDTH), row), pl.BlockSpec((tm, AT_WIDTH), row), pl.BlockSpec((tm, d), row),
                  pl.BlockSpec((ML_WIDTH, d), const), pl.BlockSpec((AT_WIDTH, d), const),
                  pl.BlockSpec((1, d), const), pl.BlockSpec((1, d), const)],
        out_specs=pl.BlockSpec((tm, d), row),
        out_shape=jax.ShapeDtypeStruct((n, d), F32),
        compiler_params=_cparams(1),
        name="out_proj_ln",
    )(ml, at, x2d, w_top, w_bot, g, b)


def _ffn_kernel(x_ref, wg_ref, wu_ref, wd_ref, g_ref, b_ref, o_ref, acc_ref, xb_ref, *, alpha):
    f = pl.program_id(1)

    @pl.when(f == 0)
    def _():
        xb_ref[...] = x_ref[...].astype(BF16)
        acc_ref[...] = jnp.zeros_like(acc_ref)

    xb = xb_ref[...]
    h = _silu(_dot(xb, wg_ref[...])) * _dot(xb, wu_ref[...])
    acc_ref[...] += _dot(h.astype(BF16), wd_ref[...])

    @pl.when(f == pl.num_programs(1) - 1)
    def _():
        o_ref[...] = _layer_norm(alpha * x_ref[...] + acc_ref[...], g_ref[...], b_ref[...])


def _ffn_chunk(d_ff):
    best = LANES
    for c in range(LANES, 1408 + 1, LANES):
        if d_ff % c == 0:
            best = c
    return best


def _dense_ffn_ln(x2d, wg, wu, wd, g, b, alpha):
    n, d = x2d.shape
    d_ff = wg.shape[1]
    tm = 512
    fc = _ffn_chunk(d_ff)
    return pl.pallas_call(
        functools.partial(_ffn_kernel, alpha=alpha),
        grid=(n // tm, d_ff // fc),
        in_specs=[pl.BlockSpec((tm, d), lambda i, f: (i, 0)),
                  pl.BlockSpec((d, fc), lambda i, f: (0, f)),
                  pl.BlockSpec((d, fc), lambda i, f: (0, f)),
                  pl.BlockSpec((fc, d), lambda i, f: (f, 0)),
                  pl.BlockSpec((1, d), lambda i, f: (0, 0)),
                  pl.BlockSpec((1, d), lambda i, f: (0, 0))],
        out_specs=pl.BlockSpec((tm, d), lambda i, f: (i, 0)),
        out_shape=jax.ShapeDtypeStruct((n, d), F32),
        scratch_shapes=[pltpu.VMEM((tm, d), F32), pltpu.VMEM((tm, d), BF16)],
        compiler_params=_cparams(2),
        name="dense_ffn_ln",
    )(x2d, wg, wu, wd, g, b)


def _router_kernel(x_ref, wh_ref, wl_ref, br_ref, comb_ref, dest_ref, cnt_ref):
    tb = x_ref.shape[0]
    x = x_ref[...]
    xh = x.astype(BF16)
    xl = (x - xh.astype(F32)).astype(BF16)
    wh = wh_ref[...]
    lg = (lax.dot_general(wh, xh, NT_DIMS, preferred_element_type=F32)
          + lax.dot_general(wh, xl, NT_DIMS, preferred_element_type=F32)
          + lax.dot_general(wl_ref[...], xh, NT_DIMS, preferred_element_type=F32)) + br_ref[:, 0:1]
    e_id = lax.broadcasted_iota(I32, (N_EXPERTS, tb), 0)
    m1 = jnp.max(lg, axis=0, keepdims=True)
    i1 = jnp.min(jnp.where(lg == m1, e_id, N_EXPERTS), axis=0, keepdims=True)
    lg2 = jnp.where(e_id == i1, -jnp.inf, lg)
    m2 = jnp.max(lg2, axis=0, keepdims=True)
    i2 = jnp.min(jnp.where(lg2 == m2, e_id, N_EXPERTS), axis=0, keepdims=True)
    ex = jnp.exp(m2 - m1)
    g1 = 1.0 / (1.0 + ex)
    g2 = ex / (1.0 + ex)
    comb = jnp.where(e_id == i1, g1, 0.0) + jnp.where(e_id == i2, g2, 0.0)
    comb_ref[...] = comb
    mask = jnp.where((e_id == i1) | (e_id == i2), 1.0, 0.0)
    w = MOE_TOKEN_CHUNK
    upper = jnp.where(lax.broadcasted_iota(I32, (w, w), 0) < lax.broadcasted_iota(I32, (w, w), 1), 1.0, 0.0).astype(BF16)
    lane = lax.broadcasted_iota(I32, (N_EXPERTS, LANES), 1)
    cnt = jnp.zeros((N_EXPERTS, LANES), F32)
    off = jnp.zeros((N_EXPERTS, 1), F32)
    for c in range(tb // w):
        cnt = jnp.where(lane == c, off, cnt)
        mc = mask[:, c * w:(c + 1) * w]
        dest_ref[:, c * w:(c + 1) * w] = (_dot(mc.astype(BF16), upper) + off).astype(I32)
        off = off + jnp.sum(mc, axis=1, keepdims=True)
    cnt_ref[...] = jnp.where(lane == tb // w, off, cnt).astype(I32)


def _router(x2d, w_hi, w_lo, b_r, tb):
    n, d = x2d.shape
    nb = n // tb
    return pl.pallas_call(
        _router_kernel,
        grid=(nb,),
        in_specs=[pl.BlockSpec((tb, d), lambda i: (i, 0)),
                  pl.BlockSpec((N_EXPERTS, d), lambda i: (0, 0)),
                  pl.BlockSpec((N_EXPERTS, d), lambda i: (0, 0)),
                  pl.BlockSpec((N_EXPERTS, LANES), lambda i: (0, 0))],
        out_specs=[pl.BlockSpec((N_EXPERTS, tb), lambda i: (0, i)),
                   pl.BlockSpec((N_EXPERTS, tb), lambda i: (0, i)),
                   pl.BlockSpec((None, N_EXPERTS, LANES), lambda i: (i, 0, 0))],
        out_shape=[jax.ShapeDtypeStruct((N_EXPERTS, n), F32),
                   jax.ShapeDtypeStruct((N_EXPERTS, n), I32),
                   jax.ShapeDtypeStruct((nb, N_EXPERTS, LANES), I32)],
        compiler_params=_cparams(1),
        name="moe_router",
    )(x2d, w_hi, w_lo, b_r)


def _moe_kernel(cnt_ref, x_ref, dest_ref, comb_ref, wgu_ref, wd_ref, g_ref, b_ref, o_ref,
                xb, xe, ye, gs, *, alpha):
    sub = MOE_ROW_TILE
    cw = MOE_TOKEN_CHUNK
    tb, d = x_ref.shape
    nch = tb // cw
    i = pl.program_id(0)
    e = pl.program_id(1)
    f = pl.program_id(2)
    n_e = pl.num_programs(1)
    n_f = pl.num_programs(2)
    base = (i * n_e + e) * (nch + 1)
    cum = [cnt_ref[base + k] for k in range(nch + 1)]
    nsub = (cum[nch] + sub - 1) // sub

    @pl.when((e == 0) & (f == 0))
    def _():
        xb[...] = x_ref[...].astype(BF16)
        o_ref[...] = jnp.zeros_like(o_ref)

    def onehot(r0, ch):
        cols = slice(ch * cw, (ch + 1) * cw)
        drow = dest_ref[pl.ds(e, 1), cols]
        crow = comb_ref[pl.ds(e, 1), cols]
        slot = r0 + lax.broadcasted_iota(I32, (sub, cw), 0)
        return jnp.where((drow == slot) & (crow > 0.0), 1.0, 0.0).astype(BF16)

    def for_overlapping_chunks(r0, fn):
        for ch in range(nch):
            pl.when((cum[ch] < r0 + sub) & (cum[ch + 1] > r0))(functools.partial(fn, ch))

    @pl.when(f == 0)
    def _():
        def body(r, carry):
            r0 = pl.multiple_of(r * sub, sub)
            rows = pl.ds(r0, sub)
            xe[rows, :] = jnp.zeros((sub, d), BF16)
            ye[rows, :] = jnp.zeros((sub, d), F32)
            gs[rows, :] = jnp.zeros((sub, LANES), F32)

            def gather(ch):
                cols = slice(ch * cw, (ch + 1) * cw)
                p = onehot(r0, ch)
                xe[rows, :] += _dot(p, xb[cols, :]).astype(BF16)
                crow = comb_ref[pl.ds(e, 1), cols]
                hi = crow.astype(BF16)
                lo = (crow - hi.astype(F32)).astype(BF16)
                c2 = jnp.concatenate([hi, lo, jnp.zeros((LANES - 2, cw), BF16)], axis=0)
                gs[rows, :] += lax.dot_general(p, c2, NT_DIMS, preferred_element_type=F32)

            for_overlapping_chunks(r0, gather)
            return carry
        lax.fori_loop(0, nsub, body, 0)

    fc = wd_ref.shape[0]

    def ffn_tile(r0, m):
        rows = pl.ds(r0, m)
        gu = _dot(xe[rows, :], wgu_ref[...])
        h = _silu(gu[:, :fc]) * gu[:, fc:]
        ye[rows, :] += _dot(h.astype(BF16), wd_ref[...])

    def ffn_quad(r, carry):
        ffn_tile(pl.multiple_of(r * 4 * sub, 4 * sub), 4 * sub)
        return carry

    n_quad = nsub // 4
    lax.fori_loop(0, n_quad, ffn_quad, 0)
    rest = nsub - 4 * n_quad
    rest0 = n_quad * 4 * sub

    @pl.when(rest >= 2)
    def _():
        ffn_tile(pl.multiple_of(rest0, 2 * sub), 2 * sub)

    @pl.when(rest % 2 == 1)
    def _():
        ffn_tile(pl.multiple_of(rest0 + (rest // 2) * 2 * sub, sub), sub)

    @pl.when(f == n_f - 1)
    def _():
        def body(r, carry):
            r0 = pl.multiple_of(r * sub, sub)
            rows = pl.ds(r0, sub)
            gate = gs[rows, 0:1] + gs[rows, 1:2]
            yw = (ye[rows, :] * gate).astype(BF16)

            def scatter(ch):
                cols = slice(ch * cw, (ch + 1) * cw)
                o_ref[cols, :] += lax.dot_general(onehot(r0, ch), yw, TN_DIMS, preferred_element_type=F32)

            for_overlapping_chunks(r0, scatter)
            return carry
        lax.fori_loop(0, nsub, body, 0)

    @pl.when((e == n_e - 1) & (f == n_f - 1))
    def _():
        o_ref[...] = _layer_norm(alpha * x_ref[...] + o_ref[...], g_ref[...], b_ref[...])


def _moe_ffn_ln(x2d, counts, dest_t, comb_t, wg, wu, wd, g, b, alpha, tb):
    n, d = x2d.shape
    n_e, _, d_ff = wg.shape
    fc = 512
    n_f = d_ff // fc
    wgu = jnp.stack([wg.reshape(n_e, d, n_f, fc), wu.reshape(n_e, d, n_f, fc)], axis=3).reshape(n_e, d, 2 * d_ff)
    grid_spec = pltpu.PrefetchScalarGridSpec(
        num_scalar_prefetch=1,
        grid=(n // tb, n_e, d_ff // fc),
        in_specs=[pl.BlockSpec((tb, d), lambda i, e, f, cnt: (i, 0), pipeline_mode=pl.Buffered(1)),
                  pl.BlockSpec((n_e, tb), lambda i, e, f, cnt: (0, i)),
                  pl.BlockSpec((n_e, tb), lambda i, e, f, cnt: (0, i)),
                  pl.BlockSpec((None, d, 2 * fc), lambda i, e, f, cnt: (e, 0, f)),
                  pl.BlockSpec((None, fc, d), lambda i, e, f, cnt: (e, f, 0)),
                  pl.BlockSpec((1, d), lambda i, e, f, cnt: (0, 0)),
                  pl.BlockSpec((1, d), lambda i, e, f, cnt: (0, 0))],
        out_specs=pl.BlockSpec((tb, d), lambda i, e, f, cnt: (i, 0)),
        scratch_shapes=[pltpu.VMEM((tb, d), BF16), pltpu.VMEM((tb, d), BF16), pltpu.VMEM((tb, d), F32),
                        pltpu.VMEM((tb, LANES), F32)],
    )
    return pl.pallas_call(
        functools.partial(_moe_kernel, alpha=alpha),
        grid_spec=grid_spec,
        out_shape=jax.ShapeDtypeStruct((n, d), F32),
        compiler_params=_cparams(3),
        name="moe_ffn_ln",
    )(counts, x2d, dest_t, comb_t, wgu, wd, g, b)


def _rope_tables(s):
    half = AT_DIM // 8
    inv = ROPE_THETA ** (-jnp.arange(half, dtype=F32) / half)
    ang = jnp.arange(s).astype(F32)[:, None] * inv[None, :]
    cos = jnp.cos(ang)
    sin = jnp.sin(ang)
    ones = jnp.ones((s, AT_DIM - 2 * half), F32)
    zeros = jnp.zeros((s, AT_DIM - 2 * half), F32)
    z8 = jnp.zeros((s, half), F32)
    cos64 = jnp.concatenate([cos, cos, ones], axis=1)
    sinp64 = jnp.concatenate([z8, sin, zeros], axis=1)
    sinm64 = jnp.concatenate([-sin, z8, zeros], axis=1)
    rep = LANES // AT_DIM
    return (jnp.tile(cos64, (1, rep)), jnp.tile(sinp64, (1, rep)), jnp.tile(sinm64, (1, rep)))


def _permute_w_in(w):
    n_in = w.shape[1]
    n_ml = 4 * ML_WIDTH
    n_gate = 2 * ML_HEADS
    src = np.full((PROJ_PAD,), -1, np.int32)
    src[:n_ml] = np.arange(n_ml)
    src[n_ml:n_in - n_gate] = np.arange(n_ml + n_gate, n_in)
    src[n_in - n_gate:n_in] = np.arange(n_ml, n_ml + n_gate)
    sel = (jnp.arange(n_in, dtype=I32)[:, None] == jnp.asarray(src)[None, :]).astype(BF16)
    return jnp.dot(w.astype(BF16), sel, preferred_element_type=F32).astype(BF16)


def kernel(x, w_in, ml_conv_w, ml_conv_b, ml_i_b, ml_f_b, ml_norm_g, idx_k_norm_g, idx_k_norm_b, w_out, ln1_g, ln1_b, ln2_g, ln2_b, ffn_w_gate, ffn_w_up, ffn_w_down, moe_w_router, moe_b_router, moe_w_gate, moe_w_up, moe_w_down):
    bsz, s, d = x.shape
    depth = w_in.shape[0]
    alpha = float((2 * depth) ** 0.25)
    n = bsz * s
    cos, sinp, sinm = _rope_tables(s)
    ml_chunk = min(256, s)
    moe_tb = min(2048, n)
    zpad = lambda v, left: jnp.pad(v, (left, LANES - left - v.shape[0]))[None, :]

    x2d = x.reshape(n, d)
    for l in range(depth):
        proj = _in_proj(x2d, _permute_w_in(w_in[l]).astype(BF16)).reshape(bsz, s, PROJ_PAD)
        gate_b = zpad(jnp.concatenate([ml_i_b[l], ml_f_b[l]]), SM_MI)
        ml_out = _mlstm(proj, ml_conv_w[l], ml_conv_b[l][None, :], gate_b, ml_norm_g[l][None, :], ml_chunk)
        q_r, k_r, v_t, iq_r, ik_r, iw_t = _dsa_prep(proj, cos, sinp, sinm,
                                                    zpad(idx_k_norm_g[l], 0), zpad(idx_k_norm_b[l], 0))
        at_out = _dsa_attention(q_r, k_r, v_t, iq_r, ik_r, iw_t)
        wo = w_out[l].astype(BF16)
        x2d = _out_proj_ln(ml_out.reshape(n, ML_WIDTH), at_out.reshape(n, AT_WIDTH), x2d,
                           wo[:ML_WIDTH], wo[ML_WIDTH:], ln1_g[l][None, :], ln1_b[l][None, :], alpha)
        j = l // 2
        if l % 2 == 0:
            x2d = _dense_ffn_ln(x2d, ffn_w_gate[j].astype(BF16), ffn_w_up[j].astype(BF16),
                                ffn_w_down[j].astype(BF16), ln2_g[l][None, :], ln2_b[l][None, :], alpha)
        else:
            wr_t = moe_w_router[j].T
            wr_hi = wr_t.astype(BF16)
            wr_lo = (wr_t - wr_hi.astype(F32)).astype(BF16)
            br = jnp.broadcast_to(moe_b_router[j][:, None], (N_EXPERTS, LANES))
            comb_t, dest_t, cnt = _router(x2d, wr_hi, wr_lo, br, moe_tb)
            counts = cnt[:, :, :moe_tb // MOE_TOKEN_CHUNK + 1].reshape(-1)
            x2d = _moe_ffn_ln(x2d, counts, dest_t, comb_t, moe_w_gate[j].astype(BF16), moe_w_up[j].astype(BF16),
                              moe_w_down[j].astype(BF16), ln2_g[l][None, :], ln2_b[l][None, :], alpha, moe_tb)
    return x2d.reshape(bsz, s, d)
```

```python
import functools

import jax
import jax.numpy as jnp
import numpy as np
from jax import lax
from jax.experimental import pallas as pl
from jax.experimental.pallas import tpu as pltpu

F32 = jnp.float32
BF16 = jnp.bfloat16
I32 = jnp.int32

ML_HEADS = 4
ML_DIM = 128
ML_WIDTH = ML_HEADS * ML_DIM
CONV_W = 4
AT_HEADS = 8
AT_KV_HEADS = 2
AT_DIM = 64
AT_WIDTH = AT_HEADS * AT_DIM
KV_WIDTH = AT_KV_HEADS * AT_DIM
IDX_HEADS = 4
IDX_DIM = 64
TOPK_MAX = 256
ROPE_THETA = 500000.0
N_EXPERTS = 8
LN_EPS = 1e-5

LANES = 128
SUBLANES = 8
VMEM_LIMIT_BYTES = 56 * 1024 * 1024

COL_MQ = 0
COL_MK = 512
COL_MV = 1024
COL_MO = 1536
COL_AQ = 2048
COL_AKV = 2560
COL_IQ = 2816
COL_SMALL = 3072
PROJ_PAD = 3200
SM_IW = 64
SM_MI = 68
SM_MF = 72

MOE_TOKEN_CHUNK = 2 * LANES
MOE_ROW_TILE = LANES

NEG_BIG = -1e30
BF16_INF_PATTERN = 0x7F80
VT_ROWS = AT_DIM + 2 * SUBLANES
NT_DIMS = (((1,), (1,)), ((), ()))
TN_DIMS = (((0,), (0,)), ((), ()))


def _cparams(n_axes):
    return pltpu.CompilerParams(dimension_semantics=("arbitrary",) * n_axes,
                                vmem_limit_bytes=VMEM_LIMIT_BYTES)


def _dot(a, b):
    return jnp.dot(a, b, preferred_element_type=F32)


def _layer_norm(z, g, b):
    mu = jnp.mean(z, axis=-1, keepdims=True)
    d = z - mu
    var = jnp.mean(d * d, axis=-1, keepdims=True)
    return d * lax.rsqrt(var + LN_EPS) * g + b


def _silu(x):
    return x / (1.0 + jnp.exp(-x))


def _in_proj_kernel(x_ref, w_ref, o_ref):
    o_ref[...] = _dot(x_ref[...].astype(BF16), w_ref[...])


def _in_proj(x2d, w_bf):
    n, d = x2d.shape
    pw = w_bf.shape[1]
    tm = 512
    return pl.pallas_call(
        _in_proj_kernel,
        grid=(n // tm,),
        in_specs=[pl.BlockSpec((tm, d), lambda i: (i, 0)),
                  pl.BlockSpec((d, pw), lambda i: (0, 0))],
        out_specs=pl.BlockSpec((tm, pw), lambda i: (i, 0)),
        out_shape=jax.ShapeDtypeStruct((n, pw), F32),
        compiler_params=_cparams(1),
        name="in_proj",
    )(x2d, w_bf)


def _mlstm_kernel(q_ref, k_ref, v_ref, o_ref, sm_ref, cw_ref, cb_ref, gb_ref, ng_ref, out_ref,
                  c_st, n_st, m_st, tail):
    @pl.when(pl.program_id(1) == 0)
    def _():
        c_st[...] = jnp.zeros_like(c_st)
        n_st[...] = jnp.zeros_like(n_st)
        m_st[...] = jnp.zeros_like(m_st)
        tail[...] = jnp.zeros_like(tail)

    for bb in range(q_ref.shape[0]):
        _mlstm_chunk(q_ref.at[bb], k_ref.at[bb], v_ref.at[bb], o_ref.at[bb], sm_ref.at[bb], cw_ref, cb_ref, gb_ref,
                     ng_ref, out_ref.at[bb], c_st.at[bb], n_st.at[bb], m_st.at[bb], tail.at[bb])


def _mlstm_chunk(q_ref, k_ref, v_ref, o_ref, sm_ref, cw_ref, cb_ref, gb_ref, ng_ref, out_ref,
                 c_st, n_st, m_st, tail):
    chunk = q_ref.shape[0]

    def conv_silu(x, prev, w, b):
        cat = jnp.concatenate([prev, x], axis=0)
        y = pltpu.roll(cat, 3, 0)[SUBLANES:] * w[0:1]
        y = y + pltpu.roll(cat, 2, 0)[SUBLANES:] * w[1:2]
        y = y + pltpu.roll(cat, 1, 0)[SUBLANES:] * w[2:3]
        y = y + x * w[3:4]
        return _silu(y + b)

    xq = q_ref[...]
    xk = k_ref[...]
    cw = cw_ref[...]
    cb = cb_ref[...]
    qs = conv_silu(xq, tail[:, :ML_WIDTH], cw[:, :ML_WIDTH], cb[:, :ML_WIDTH]) * (ML_DIM ** -0.5)
    ks = conv_silu(xk, tail[:, ML_WIDTH:], cw[:, ML_WIDTH:], cb[:, ML_WIDTH:])
    tail[:, :ML_WIDTH] = xq[chunk - SUBLANES:]
    tail[:, ML_WIDTH:] = xk[chunk - SUBLANES:]

    gates = sm_ref[...] + gb_ref[...]
    logf = jnp.minimum(gates, 0.0) - jnp.log1p(jnp.exp(-jnp.abs(gates)))
    row = lax.broadcasted_iota(I32, (chunk, chunk), 0)
    col = lax.broadcasted_iota(I32, (chunk, chunk), 1)
    causal = row >= col
    tri = jnp.where(causal, 1.0, 0.0).astype(BF16)
    hi = logf.astype(BF16)
    r1 = logf - hi.astype(F32)
    mid = r1.astype(BF16)
    lo = (r1 - mid.astype(F32)).astype(BF16)
    bcum = _dot(tri, hi) + _dot(tri, mid) + _dot(tri, lo)
    gates_t = gates.T
    bcum_t = bcum.T

    for h in range(ML_HEADS):
        sl = slice(h * ML_DIM, (h + 1) * ML_DIM)
        qh = qs[:, sl]
        kh = ks[:, sl]
        vb = v_ref[:, sl].astype(BF16)
        b = bcum[:, SM_MF + h:SM_MF + h + 1]
        li = gates[:, SM_MI + h:SM_MI + h + 1]
        r = gates_t[SM_MI + h:SM_MI + h + 1, :] - bcum_t[SM_MF + h:SM_MF + h + 1, :]
        m_prev = m_st[h][0:1, 0:1]
        dmat = jnp.where(causal, b + r, -jnp.inf)
        g_inter = b + m_prev
        m_t = jnp.maximum(g_inter, jnp.max(dmat, axis=1, keepdims=True))
        w_inter = jnp.exp(g_inter - m_t)
        qb = qh.astype(BF16)
        kb = kh.astype(BF16)
        sc = lax.dot_general(qb, kb, NT_DIMS, preferred_element_type=F32) * jnp.exp(dmat - m_t)
        c_old = c_st[h]
        n_old = n_st[h][0:1, :]
        num = w_inter * _dot(qb, c_old.astype(BF16)) + _dot(sc.astype(BF16), vb)
        den = w_inter * jnp.sum(qh * n_old, axis=1, keepdims=True) + jnp.sum(sc, axis=1, keepdims=True)
        hh = num / jnp.maximum(jnp.abs(den), jnp.exp(-m_t))
        b_last = b[chunk - 1:chunk, :]
        g_state = b_last + m_prev
        ls = b_last - b + li
        m_new = jnp.maximum(g_state, jnp.max(ls, axis=0, keepdims=True))
        ws = jnp.exp(ls - m_new)
        decay = jnp.exp(g_state - m_new)
        kw = kh * ws
        c_st[h] = decay * c_old + _dot(kw.T.astype(BF16), vb)
        n_st[h] = jnp.broadcast_to(decay * n_old + jnp.sum(kw, axis=0, keepdims=True), (SUBLANES, ML_DIM))
        m_st[h] = jnp.broadcast_to(m_new, (SUBLANES, LANES))
        mu = jnp.mean(hh, axis=1, keepdims=True)
        d = hh - mu
        var = jnp.mean(d * d, axis=1, keepdims=True)
        hn = d * lax.rsqrt(var + LN_EPS) * ng_ref[:, sl]
        out_ref[:, sl] = (hn / (1.0 + jnp.exp(-o_ref[:, sl]))).astype(out_ref.dtype)


def _mlstm(proj, conv_w, conv_b, gate_b, norm_g, chunk):
    bsz, s, _ = proj.shape
    wblk = ML_WIDTH
    nb = 1

    def colspec(col, width):
        return pl.BlockSpec((nb, chunk, width), lambda b, c: (b, c, col // width))

    def full2d(shape):
        return pl.BlockSpec(shape, lambda b, c: (0, 0))

    return pl.pallas_call(
        _mlstm_kernel,
        grid=(bsz // nb, s // chunk),
        in_specs=[colspec(COL_MQ, wblk), colspec(COL_MK, wblk), colspec(COL_MV, wblk), colspec(COL_MO, wblk),
                  colspec(COL_SMALL, LANES),
                  full2d((CONV_W, 2 * ML_WIDTH)), full2d((1, 2 * ML_WIDTH)), full2d((1, LANES)),
                  full2d((1, ML_WIDTH))],
        out_specs=pl.BlockSpec((nb, chunk, ML_WIDTH), lambda b, c: (b, c, 0)),
        out_shape=jax.ShapeDtypeStruct((bsz, s, ML_WIDTH), BF16),
        scratch_shapes=[pltpu.VMEM((nb, ML_HEADS, ML_DIM, ML_DIM), F32),
                        pltpu.VMEM((nb, ML_HEADS, SUBLANES, ML_DIM), F32),
                        pltpu.VMEM((nb, ML_HEADS, SUBLANES, LANES), F32),
                        pltpu.VMEM((nb, SUBLANES, 2 * ML_WIDTH), F32)],
        compiler_params=_cparams(2),
        name="mlstm",
    )(proj, proj, proj, proj, proj, conv_w, conv_b, gate_b, norm_g)


def _rope(x, cos, sinp, sinm):
    width = x.shape[1]
    reps = width // LANES
    if reps > 1:
        cos = jnp.concatenate([cos] * reps, axis=1)
        sinp = jnp.concatenate([sinp] * reps, axis=1)
        sinm = jnp.concatenate([sinm] * reps, axis=1)
    half = AT_DIM // 8
    return x * cos + pltpu.roll(x, half, 1) * sinp + pltpu.roll(x, width - half, 1) * sinm


def _dsa_prep_kernel(aq_ref, akv_ref, iq_ref, sm_ref, cos_ref, sinp_ref, sinm_ref, lng_ref, lnb_ref,
                     qt_out, k_out, vt_out, iqt_out, ik_out, iwt_out):
    cos = cos_ref[...]
    sinp = sinp_ref[...]
    sinm = sinm_ref[...]
    q = _rope(aq_ref[...], cos, sinp, sinm) * (AT_DIM ** -0.5)
    sm = sm_ref[...]
    lane = lax.broadcasted_iota(I32, sm.shape, 1)
    rep = AT_HEADS // AT_KV_HEADS
    for pair in range(AT_HEADS // 2):
        slab = q[:, pair * LANES:(pair + 1) * LANES]
        swapped = pltpu.roll(slab, AT_DIM, 1)
        for p in range(2):
            h = 2 * pair + p
            g = h // rep
            src = slab if p == g else swapped
            in_group = (lane >= g * AT_DIM) & (lane < (g + 1) * AT_DIM)
            qt_out[h * LANES:(h + 1) * LANES, :] = jnp.where(in_group, src, 0.0).T.astype(BF16)
    akv = akv_ref[...]
    k_out[...] = _rope(akv[:, :KV_WIDTH], cos, sinp, sinm).astype(BF16)
    v_t = akv[:, KV_WIDTH:].T
    t_len = v_t.shape[1]
    tail = jnp.where(lax.broadcasted_iota(I32, (VT_ROWS - AT_DIM, t_len), 0) == 0, 1.0, 0.0)
    for g in range(AT_KV_HEADS):
        vt_out[g] = jnp.concatenate([v_t[g * AT_DIM:(g + 1) * AT_DIM], tail], axis=0).astype(BF16)
    iqt_out[...] = (_rope(iq_ref[...], cos, sinp, sinm) * (IDX_DIM ** -0.5)).T.astype(BF16)
    is_k = lane < IDX_DIM
    mu = jnp.sum(jnp.where(is_k, sm, 0.0), axis=1, keepdims=True) * (1.0 / IDX_DIM)
    d = jnp.where(is_k, sm - mu, 0.0)
    var = jnp.sum(d * d, axis=1, keepdims=True) * (1.0 / IDX_DIM)
    y = jnp.where(is_k, d * lax.rsqrt(var + LN_EPS) * lng_ref[...] + lnb_ref[...], 0.0)
    ik_out[...] = _rope(y, cos, sinp, sinm).astype(BF16)
    iwt_out[...] = sm.T[SM_IW:SM_IW + SUBLANES, :] * (IDX_HEADS ** -0.5)


def _dsa_prep(proj, cos, sinp, sinm, ln_g, ln_b):
    bsz, s, _ = proj.shape
    t = 512

    def colspec(col, width):
        return pl.BlockSpec((None, t, width), lambda b, c: (b, c, col // width))

    tab = pl.BlockSpec((t, LANES), lambda b, c: (c, 0))
    vec = pl.BlockSpec((1, LANES), lambda b, c: (0, 0))

    def outspec(width):
        return pl.BlockSpec((None, t, width), lambda b, c: (b, c, 0))

    return pl.pallas_call(
        _dsa_prep_kernel,
        grid=(bsz, s // t),
        in_specs=[colspec(COL_AQ, AT_WIDTH), colspec(COL_AKV, 2 * KV_WIDTH), colspec(COL_IQ, IDX_HEADS * IDX_DIM),
                  colspec(COL_SMALL, LANES), tab, tab, tab, vec, vec],
        out_specs=[pl.BlockSpec((None, AT_HEADS * LANES, t), lambda b, c: (b, 0, c)),
                   outspec(KV_WIDTH),
                   pl.BlockSpec((None, AT_KV_HEADS, VT_ROWS, t), lambda b, c: (b, 0, 0, c)),
                   pl.BlockSpec((None, IDX_HEADS * IDX_DIM, t), lambda b, c: (b, 0, c)),
                   outspec(LANES),
                   pl.BlockSpec((None, SUBLANES, t), lambda b, c: (b, 0, c))],
        out_shape=[jax.ShapeDtypeStruct((bsz, AT_HEADS * LANES, s), BF16),
                   jax.ShapeDtypeStruct((bsz, s, KV_WIDTH), BF16),
                   jax.ShapeDtypeStruct((bsz, AT_KV_HEADS, VT_ROWS, s), BF16),
                   jax.ShapeDtypeStruct((bsz, IDX_HEADS * IDX_DIM, s), BF16),
                   jax.ShapeDtypeStruct((bsz, s, LANES), BF16),
                   jax.ShapeDtypeStruct((bsz, SUBLANES, s), F32)],
        compiler_params=_cparams(2),
        name="dsa_prep",
    )(proj, proj, proj, proj, cos, sinp, sinm, ln_g, ln_b)


def _tree_sum(parts):
    while len(parts) > 1:
        parts = [parts[i] + parts[i + 1] for i in range(0, len(parts) - 1, 2)] + ([parts[-1]] if len(parts) % 2 else [])
    return parts[0]


def _dsa_kernel(qt_ref, iqt_ref, iwt_ref, k_ref, vt_ref, ik_ref, out_ref, score, coarse, bias, acc, m_s, x_s,
                *, topk, seq, ck):
    tq = qt_ref.shape[1]
    j = pl.program_id(1)
    nck = (j * tq + tq + ck - 1) // ck
    t_abs = j * tq + lax.broadcasted_iota(I32, (ck, tq), 1)
    s_loc = lax.broadcasted_iota(I32, (ck, tq), 0)
    rep = AT_HEADS // AT_KV_HEADS
    pack = 2 * SUBLANES

    iqt = iqt_ref[...]
    iwt = iwt_ref[...]
    iq_all = jnp.concatenate([iqt[h * IDX_DIM:(h + 1) * IDX_DIM, :] for h in range(IDX_HEADS)], axis=1)
    iw_all = jnp.concatenate([iwt[h:h + 1, :] for h in range(IDX_HEADS)], axis=1)

    def chunk_off(c):
        return pl.multiple_of(c * ck, ck)

    def score_body(c, carry):
        off = chunk_off(c)
        kic = ik_ref[pl.ds(off, ck), :][:, :IDX_DIM]
        sw = jnp.maximum(_dot(kic, iq_all), 0.0) * iw_all
        s = sw[:, :tq]
        for h in range(1, IDX_HEADS):
            s = s + sw[:, h * tq:(h + 1) * tq]
        s = jnp.where(off + s_loc <= t_abs, s, -jnp.inf)
        s = jnp.where(s == 0.0, 0.0, s)
        score[pl.ds(off, ck), :] = s
        coarse[pl.ds(off, ck), :] = s.astype(BF16)
        return carry

    lax.fori_loop(0, nck, score_body, 0)

    def count(pred):
        def body(c, a):
            off = chunk_off(c)
            m = jnp.where(pred(score[pl.ds(off, ck), :], off), 1, 0)
            return a + jnp.sum(m.reshape(ck // SUBLANES, SUBLANES, tq), axis=0)
        a = lax.fori_loop(0, nck, body, jnp.zeros((SUBLANES, tq), I32))
        return jnp.sum(a, axis=0, keepdims=True)

    def count_coarse(pred):
        one = jnp.ones((), BF16)
        zero = jnp.zeros((), BF16)

        def body(c, a):
            m = jnp.where(pred(coarse[pl.ds(chunk_off(c), ck), :]), one, zero).reshape(ck // pack, pack, tq)
            return a + _tree_sum([m[i] for i in range(ck // pack)]).astype(F32)
        a = lax.fori_loop(0, nck, body, jnp.zeros((pack, tq), F32))
        return jnp.sum(a, axis=0, keepdims=True).astype(I32)

    def key_to_f32(k):
        return pltpu.bitcast(k ^ ((k >> 31) & jnp.int32(0x7FFFFFFF)), F32)

    def key16_to_bf16(k):
        pattern = (k ^ ((k >> 15) & 0x7FFF)) & 0xFFFF
        return pltpu.bitcast(pattern << 16, F32).astype(BF16)

    lo16 = -(1 << 15)
    t16 = jnp.full((1, tq), lo16, I32)
    for bit in reversed(range(16)):
        cand = t16 + (1 << bit)
        cand_b = key16_to_bf16(cand)
        c = count_coarse(lambda hc, cand_b=cand_b: hc >= cand_b)
        t16 = jnp.where(c >= topk, cand, t16)
    key_lo = jnp.clip(t16 - 1, lo16, BF16_INF_PATTERN - 1) << 16
    offs = jnp.zeros((1, tq), I32)
    for bit in reversed(range(18)):
        cand = offs + (1 << bit)
        cand_f = key_to_f32(key_lo + cand)
        c = count(lambda sc, off, cand_f=cand_f: sc >= cand_f)
        offs = jnp.where(c >= topk, cand, offs)
    thr = key_to_f32(key_lo + offs)

    take_all = t_abs[0:1, :] < topk
    n_gt = count(lambda sc, off: sc > thr)
    n_eq = count(lambda sc, off: sc == thr)
    need = topk - n_gt
    excess = (n_eq > need) & jnp.logical_not(take_all)
    x_s[...] = jnp.full(x_s.shape, seq, I32)

    any_excess = jnp.max(jnp.where(excess, 1, 0)) > 0

    @pl.when(any_excess)
    def _():
        x = jnp.zeros((1, tq), I32)
        for bit in reversed(range(max(seq - 1, 1).bit_length())):
            cand = x + (1 << bit)
            c = count(lambda sc, off: (sc == thr) & (off + s_loc < cand))
            x = jnp.where(c < need, cand, x)
        x_s[...] = jnp.broadcast_to(jnp.where(excess, x, seq), x_s.shape)

    def write_bias(with_ties):
        xlim = x_s[0:1, :]

        def bias_body(c, carry):
            off = chunk_off(c)
            sc = score[pl.ds(off, ck), :]
            s_abs = off + s_loc
            if with_ties:
                sel = (sc > thr) | ((sc == thr) & (s_abs <= xlim)) | take_all
            else:
                sel = (sc >= thr) | take_all
            bias[pl.ds(off, ck), :] = jnp.where(sel & (s_abs <= t_abs), 0.0, NEG_BIG)
            return carry

        lax.fori_loop(0, nck, bias_body, 0)

    pl.when(any_excess)(functools.partial(write_bias, True))
    pl.when(jnp.logical_not(any_excess))(functools.partial(write_bias, False))

    m_s[...] = jnp.full(m_s.shape, NEG_BIG, F32)
    acc[...] = jnp.zeros_like(acc)

    def att_body(c, carry):
        off = chunk_off(c)
        kc = k_ref[pl.ds(off, ck), :]
        bc = bias[pl.ds(off, ck), :]
        b_all = jnp.concatenate([bc] * rep, axis=1)
        for g in range(AT_KV_HEADS):
            qg = jnp.concatenate([qt_ref[(g * rep + r) * LANES:(g * rep + r + 1) * LANES, :] for r in range(rep)],
                                 axis=1)
            lg = _dot(kc, qg) + b_all
            m_old = m_s[g]
            m_new = jnp.maximum(m_old, jnp.max(lg, axis=0, keepdims=True))
            a = jnp.exp(m_old - m_new)
            p = jnp.exp(lg - m_new)
            acc[g] = a * acc[g] + _dot(vt_ref[g, :, pl.ds(off, ck)], p.astype(BF16))
            m_s[g] = m_new
        return carry

    lax.fori_loop(0, nck, att_body, 0)

    for g in range(AT_KV_HEADS):
        ag = acc[g]
        on = ag[:AT_DIM] / ag[AT_DIM:AT_DIM + 1]
        for pp in range(rep // 2):
            two = jnp.concatenate([on[:, 2 * pp * tq:(2 * pp + 1) * tq], on[:, (2 * pp + 1) * tq:(2 * pp + 2) * tq]],
                                  axis=0)
            pair = g * (rep // 2) + pp
            out_ref[:, pair * LANES:(pair + 1) * LANES] = two.T.astype(out_ref.dtype)


def _dsa_attention(q_t, k_r, v_t, iq_t, ik_r, iw_t):
    bsz, s, _ = k_r.shape
    tq = min(512, s)
    topk = min(TOPK_MAX, s // 4)
    rep = AT_HEADS // AT_KV_HEADS
    ck = min(512, s)
    assert s % ck == 0 and ck >= topk and ck % tq == 0
    kern = functools.partial(_dsa_kernel, topk=topk, seq=s, ck=ck)
    return pl.pallas_call(
        kern,
        grid=(bsz, s // tq),
        in_specs=[pl.BlockSpec((None, AT_HEADS * LANES, tq), lambda b, j: (b, 0, j)),
                  pl.BlockSpec((None, IDX_HEADS * IDX_DIM, tq), lambda b, j: (b, 0, j)),
                  pl.BlockSpec((None, SUBLANES, tq), lambda b, j: (b, 0, j)),
                  pl.BlockSpec((None, s, KV_WIDTH), lambda b, j: (b, 0, 0)),
                  pl.BlockSpec((None, AT_KV_HEADS, VT_ROWS, s), lambda b, j: (b, 0, 0, 0)),
                  pl.BlockSpec((None, s, LANES), lambda b, j: (b, 0, 0))],
        out_specs=pl.BlockSpec((None, tq, AT_WIDTH), lambda b, j: (b, j, 0)),
        out_shape=jax.ShapeDtypeStruct((bsz, s, AT_WIDTH), BF16),
        scratch_shapes=[pltpu.VMEM((s, tq), F32),
                        pltpu.VMEM((s, tq), BF16),
                        pltpu.VMEM((s, tq), F32),
                        pltpu.VMEM((AT_KV_HEADS, VT_ROWS, rep * tq), F32),
                        pltpu.VMEM((AT_KV_HEADS, 1, rep * tq), F32),
                        pltpu.VMEM((SUBLANES, tq), I32)],
        compiler_params=_cparams(2),
        name="dsa_attn",
    )(q_t, iq_t, iw_t, k_r, v_t, ik_r)


def _out_ln_kernel(ml_ref, at_ref, x_ref, wt_ref, wb_ref, g_ref, b_ref, o_ref, *, alpha):
    mix = _dot(ml_ref[...], wt_ref[...]) + _dot(at_ref[...], wb_ref[...])
    o_ref[...] = _layer_norm(alpha * x_ref[...] + mix, g_ref[...], b_ref[...])


def _out_proj_ln(ml, at, x2d, w_top, w_bot, g, b, alpha):
    n, d = x2d.shape
    tm = 512
    row = lambda i: (i, 0)
    const = lambda i: (0, 0)
    return pl.pallas_call(
        functools.partial(_out_ln_kernel, alpha=alpha),
        grid=(n // tm,),
        in_specs=[pl.BlockSpec((tm, ML_WIDTH), row), pl.BlockSpec((tm, AT_WIDTH), row), pl.BlockSpec((tm, d), row),
                  pl.BlockSpec((ML_WIDTH, d), const), pl.BlockSpec((AT_WIDTH, d), const),
                  pl.BlockSpec((1, d), const), pl.BlockSpec((1, d), const)],
        out_specs=pl.BlockSpec((tm, d), row),
        out_shape=jax.ShapeDtypeStruct((n, d), F32),
        compiler_params=_cparams(1),
        name="out_proj_ln",
    )(ml, at, x2d, w_top, w_bot, g, b)


def _ffn_kernel(x_ref, wg_ref, wu_ref, wd_ref, g_ref, b_ref, o_ref, acc_ref, xb_ref, *, alpha):
    f = pl.program_id(1)

    @pl.when(f == 0)
    def _():
        xb_ref[...] = x_ref[...].astype(BF16)
        acc_ref[...] = jnp.zeros_like(acc_ref)

    xb = xb_ref[...]
    h = _silu(_dot(xb, wg_ref[...])) * _dot(xb, wu_ref[...])
    acc_ref[...] += _dot(h.astype(BF16), wd_ref[...])

    @pl.when(f == pl.num_programs(1) - 1)
    def _():
        o_ref[...] = _layer_norm(alpha * x_ref[...] + acc_ref[...], g_ref[...], b_ref[...])


def _ffn_chunk(d_ff):
    best = LANES
    for c in range(LANES, 1408 + 1, LANES):
        if d_ff % c == 0:
            best = c
    return best


def _dense_ffn_ln(x2d, wg, wu, wd, g, b, alpha):
    n, d = x2d.shape
    d_ff = wg.shape[1]
    tm = 512
    fc = _ffn_chunk(d_ff)
    return pl.pallas_call(
        functools.partial(_ffn_kernel, alpha=alpha),
        grid=(n // tm, d_ff // fc),
        in_specs=[pl.BlockSpec((tm, d), lambda i, f: (i, 0)),
                  pl.BlockSpec((d, fc), lambda i, f: (0, f)),
                  pl.BlockSpec((d, fc), lambda i, f: (0, f)),
                  pl.BlockSpec((fc, d), lambda i, f: (f, 0)),
                  pl.BlockSpec((1, d), lambda i, f: (0, 0)),
                  pl.BlockSpec((1, d), lambda i, f: (0, 0))],
        out_specs=pl.BlockSpec((tm, d), lambda i, f: (i, 0)),
        out_shape=jax.ShapeDtypeStruct((n, d), F32),
        scratch_shapes=[pltpu.VMEM((tm, d), F32), pltpu.VMEM((tm, d), BF16)],
        compiler_params=_cparams(2),
        name="dense_ffn_ln",
    )(x2d, wg, wu, wd, g, b)


def _router_kernel(x_ref, wh_ref, wl_ref, br_ref, comb_ref, dest_ref, cnt_ref):
    tb = x_ref.shape[0]
    x = x_ref[...]
    xh = x.astype(BF16)
    xl = (x - xh.astype(F32)).astype(BF16)
    wh = wh_ref[...]
    lg = (lax.dot_general(wh, xh, NT_DIMS, preferred_element_type=F32)
          + lax.dot_general(wh, xl, NT_DIMS, preferred_element_type=F32)
          + lax.dot_general(wl_ref[...], xh, NT_DIMS, preferred_element_type=F32)) + br_ref[:, 0:1]
    e_id = lax.broadcasted_iota(I32, (N_EXPERTS, tb), 0)
    m1 = jnp.max(lg, axis=0, keepdims=True)
    i1 = jnp.min(jnp.where(lg == m1, e_id, N_EXPERTS), axis=0, keepdims=True)
    lg2 = jnp.where(e_id == i1, -jnp.inf, lg)
    m2 = jnp.max(lg2, axis=0, keepdims=True)
    i2 = jnp.min(jnp.where(lg2 == m2, e_id, N_EXPERTS), axis=0, keepdims=True)
    ex = jnp.exp(m2 - m1)
    g1 = 1.0 / (1.0 + ex)
    g2 = ex / (1.0 + ex)
    comb = jnp.where(e_id == i1, g1, 0.0) + jnp.where(e_id == i2, g2, 0.0)
    comb_ref[...] = comb
    mask = jnp.where((e_id == i1) | (e_id == i2), 1.0, 0.0)
    w = MOE_TOKEN_CHUNK
    upper = jnp.where(lax.broadcasted_iota(I32, (w, w), 0) < lax.broadcasted_iota(I32, (w, w), 1), 1.0, 0.0).astype(BF16)
    lane = lax.broadcasted_iota(I32, (N_EXPERTS, LANES), 1)
    cnt = jnp.zeros((N_EXPERTS, LANES), F32)
    off = jnp.zeros((N_EXPERTS, 1), F32)
    for c in range(tb // w):
        cnt = jnp.where(lane == c, off, cnt)
        mc = mask[:, c * w:(c + 1) * w]
        dest_ref[:, c * w:(c + 1) * w] = (_dot(mc.astype(BF16), upper) + off).astype(I32)
        off = off + jnp.sum(mc, axis=1, keepdims=True)
    cnt_ref[...] = jnp.where(lane == tb // w, off, cnt).astype(I32)


def _router(x2d, w_hi, w_lo, b_r, tb):
    n, d = x2d.shape
    nb = n // tb
    return pl.pallas_call(
        _router_kernel,
        grid=(nb,),
        in_specs=[pl.BlockSpec((tb, d), lambda i: (i, 0)),
                  pl.BlockSpec((N_EXPERTS, d), lambda i: (0, 0)),
                  pl.BlockSpec((N_EXPERTS, d), lambda i: (0, 0)),
                  pl.BlockSpec((N_EXPERTS, LANES), lambda i: (0, 0))],
        out_specs=[pl.BlockSpec((N_EXPERTS, tb), lambda i: (0, i)),
                   pl.BlockSpec((N_EXPERTS, tb), lambda i: (0, i)),
                   pl.BlockSpec((None, N_EXPERTS, LANES), lambda i: (i, 0, 0))],
        out_shape=[jax.ShapeDtypeStruct((N_EXPERTS, n), F32),
                   jax.ShapeDtypeStruct((N_EXPERTS, n), I32),
                   jax.ShapeDtypeStruct((nb, N_EXPERTS, LANES), I32)],
        compiler_params=_cparams(1),
        name="moe_router",
    )(x2d, w_hi, w_lo, b_r)


def _moe_kernel(cnt_ref, x_ref, dest_ref, comb_ref, wg_ref, wu_ref, wd_ref, g_ref, b_ref, o_ref,
                xb, xe, ye, gs, *, alpha):
    sub = MOE_ROW_TILE
    cw = MOE_TOKEN_CHUNK
    tb, d = x_ref.shape
    nch = tb // cw
    i = pl.program_id(0)
    e = pl.program_id(1)
    f = pl.program_id(2)
    n_e = pl.num_programs(1)
    n_f = pl.num_programs(2)
    base = (i * n_e + e) * (nch + 1)
    cum = [cnt_ref[base + k] for k in range(nch + 1)]
    nsub = (cum[nch] + sub - 1) // sub

    @pl.when((e == 0) & (f == 0))
    def _():
        xb[...] = x_ref[...].astype(BF16)
        o_ref[...] = jnp.zeros_like(o_ref)

    def onehot(r0, ch):
        cols = slice(ch * cw, (ch + 1) * cw)
        drow = dest_ref[pl.ds(e, 1), cols]
        crow = comb_ref[pl.ds(e, 1), cols]
        slot = r0 + lax.broadcasted_iota(I32, (sub, cw), 0)
        return jnp.where((drow == slot) & (crow > 0.0), 1.0, 0.0).astype(BF16)

    def for_overlapping_chunks(r0, fn):
        for ch in range(nch):
            pl.when((cum[ch] < r0 + sub) & (cum[ch + 1] > r0))(functools.partial(fn, ch))

    @pl.when(f == 0)
    def _():
        def body(r, carry):
            r0 = pl.multiple_of(r * sub, sub)
            rows = pl.ds(r0, sub)
            xe[rows, :] = jnp.zeros((sub, d), BF16)
            ye[rows, :] = jnp.zeros((sub, d), F32)
            gs[rows, :] = jnp.zeros((sub, LANES), F32)

            def gather(ch):
                cols = slice(ch * cw, (ch + 1) * cw)
                p = onehot(r0, ch)
                xe[rows, :] += _dot(p, xb[cols, :]).astype(BF16)
                crow = comb_ref[pl.ds(e, 1), cols]
                hi = crow.astype(BF16)
                lo = (crow - hi.astype(F32)).astype(BF16)
                c2 = jnp.concatenate([hi, lo, jnp.zeros((LANES - 2, cw), BF16)], axis=0)
                gs[rows, :] += lax.dot_general(p, c2, NT_DIMS, preferred_element_type=F32)

            for_overlapping_chunks(r0, gather)
            return carry
        lax.fori_loop(0, nsub, body, 0)

    def ffn_tile(r0, m):
        rows = pl.ds(r0, m)
        xr = xe[rows, :]
        h = _silu(_dot(xr, wg_ref[...])) * _dot(xr, wu_ref[...])
        ye[rows, :] += _dot(h.astype(BF16), wd_ref[...])

    def ffn_quad(r, carry):
        ffn_tile(pl.multiple_of(r * 4 * sub, 4 * sub), 4 * sub)
        return carry

    n_quad = nsub // 4
    lax.fori_loop(0, n_quad, ffn_quad, 0)
    rest = nsub - 4 * n_quad
    rest0 = n_quad * 4 * sub

    @pl.when(rest >= 2)
    def _():
        ffn_tile(pl.multiple_of(rest0, 2 * sub), 2 * sub)

    @pl.when(rest % 2 == 1)
    def _():
        ffn_tile(pl.multiple_of(rest0 + (rest // 2) * 2 * sub, sub), sub)

    @pl.when(f == n_f - 1)
    def _():
        def body(r, carry):
            r0 = pl.multiple_of(r * sub, sub)
            rows = pl.ds(r0, sub)
            gate = gs[rows, 0:1] + gs[rows, 1:2]
            yw = (ye[rows, :] * gate).astype(BF16)

            def scatter(ch):
                cols = slice(ch * cw, (ch + 1) * cw)
                o_ref[cols, :] += lax.dot_general(onehot(r0, ch), yw, TN_DIMS, preferred_element_type=F32)

            for_overlapping_chunks(r0, scatter)
            return carry
        lax.fori_loop(0, nsub, body, 0)

    @pl.when((e == n_e - 1) & (f == n_f - 1))
    def _():
        o_ref[...] = _layer_norm(alpha * x_ref[...] + o_ref[...], g_ref[...], b_ref[...])


def _moe_ffn_ln(x2d, counts, dest_t, comb_t, wg, wu, wd, g, b, alpha, tb):
    n, d = x2d.shape
    n_e, _, d_ff = wg.shape
    fc = 512
    grid_spec = pltpu.PrefetchScalarGridSpec(
        num_scalar_prefetch=1,
        grid=(n // tb, n_e, d_ff // fc),
        in_specs=[pl.BlockSpec((tb, d), lambda i, e, f, cnt: (i, 0), pipeline_mode=pl.Buffered(1)),
                  pl.BlockSpec((n_e, tb), lambda i, e, f, cnt: (0, i)),
                  pl.BlockSpec((n_e, tb), lambda i, e, f, cnt: (0, i)),
                  pl.BlockSpec((None, d, fc), lambda i, e, f, cnt: (e, 0, f)),
                  pl.BlockSpec((None, d, fc), lambda i, e, f, cnt: (e, 0, f)),
                  pl.BlockSpec((None, fc, d), lambda i, e, f, cnt: (e, f, 0)),
                  pl.BlockSpec((1, d), lambda i, e, f, cnt: (0, 0)),
                  pl.BlockSpec((1, d), lambda i, e, f, cnt: (0, 0))],
        out_specs=pl.BlockSpec((tb, d), lambda i, e, f, cnt: (i, 0)),
        scratch_shapes=[pltpu.VMEM((tb, d), BF16), pltpu.VMEM((tb, d), BF16), pltpu.VMEM((tb, d), F32),
                        pltpu.VMEM((tb, LANES), F32)],
    )
    return pl.pallas_call(
        functools.partial(_moe_kernel, alpha=alpha),
        grid_spec=grid_spec,
        out_shape=jax.ShapeDtypeStruct((n, d), F32),
        compiler_params=_cparams(3),
        name="moe_ffn_ln",
    )(counts, x2d, dest_t, comb_t, wg, wu, wd, g, b)


def _rope_tables(s):
    half = AT_DIM // 8
    inv = ROPE_THETA ** (-jnp.arange(half, dtype=F32) / half)
    ang = jnp.arange(s).astype(F32)[:, None] * inv[None, :]
    cos = jnp.cos(ang)
    sin = jnp.sin(ang)
    ones = jnp.ones((s, AT_DIM - 2 * half), F32)
    zeros = jnp.zeros((s, AT_DIM - 2 * half), F32)
    z8 = jnp.zeros((s, half), F32)
    cos64 = jnp.concatenate([cos, cos, ones], axis=1)
    sinp64 = jnp.concatenate([z8, sin, zeros], axis=1)
    sinm64 = jnp.concatenate([-sin, z8, zeros], axis=1)
    rep = LANES // AT_DIM
    return (jnp.tile(cos64, (1, rep)), jnp.tile(sinp64, (1, rep)), jnp.tile(sinm64, (1, rep)))


def _permute_w_in(w):
    n_in = w.shape[1]
    n_ml = 4 * ML_WIDTH
    n_gate = 2 * ML_HEADS
    src = np.full((PROJ_PAD,), -1, np.int32)
    src[:n_ml] = np.arange(n_ml)
    src[n_ml:n_in - n_gate] = np.arange(n_ml + n_gate, n_in)
    src[n_in - n_gate:n_in] = np.arange(n_ml, n_ml + n_gate)
    sel = (jnp.arange(n_in, dtype=I32)[:, None] == jnp.asarray(src)[None, :]).astype(BF16)
    return jnp.dot(w.astype(BF16), sel, preferred_element_type=F32).astype(BF16)


def kernel(x, w_in, ml_conv_w, ml_conv_b, ml_i_b, ml_f_b, ml_norm_g, idx_k_norm_g, idx_k_norm_b, w_out, ln1_g, ln1_b, ln2_g, ln2_b, ffn_w_gate, ffn_w_up, ffn_w_down, moe_w_router, moe_b_router, moe_w_gate, moe_w_up, moe_w_down):
    bsz, s, d = x.shape
    depth = w_in.shape[0]
    alpha = float((2 * depth) ** 0.25)
    n = bsz * s
    cos, sinp, sinm = _rope_tables(s)
    ml_chunk = min(256, s)
    moe_tb = min(2048, n)
    zpad = lambda v, left: jnp.pad(v, (left, LANES - left - v.shape[0]))[None, :]

    x2d = x.reshape(n, d)
    for l in range(depth):
        proj = _in_proj(x2d, _permute_w_in(w_in[l]).astype(BF16)).reshape(bsz, s, PROJ_PAD)
        gate_b = zpad(jnp.concatenate([ml_i_b[l], ml_f_b[l]]), SM_MI)
        ml_out = _mlstm(proj, ml_conv_w[l], ml_conv_b[l][None, :], gate_b, ml_norm_g[l][None, :], ml_chunk)
        q_r, k_r, v_t, iq_r, ik_r, iw_t = _dsa_prep(proj, cos, sinp, sinm,
                                                    zpad(idx_k_norm_g[l], 0), zpad(idx_k_norm_b[l], 0))
        at_out = _dsa_attention(q_r, k_r, v_t, iq_r, ik_r, iw_t)
        wo = w_out[l].astype(BF16)
        x2d = _out_proj_ln(ml_out.reshape(n, ML_WIDTH), at_out.reshape(n, AT_WIDTH), x2d,
                           wo[:ML_WIDTH], wo[ML_WIDTH:], ln1_g[l][None, :], ln1_b[l][None, :], alpha)
        j = l // 2
        if l % 2 == 0:
            x2d = _dense_ffn_ln(x2d, ffn_w_gate[j].astype(BF16), ffn_w_up[j].astype(BF16),
                                ffn_w_down[j].astype(BF16), ln2_g[l][None, :], ln2_b[l][None, :], alpha)
        else:
            wr_t = moe_w_router[j].T
            wr_hi = wr_t.astype(BF16)
            wr_lo = (wr_t - wr_hi.astype(F32)).astype(BF16)
            br = jnp.broadcast_to(moe_b_router[j][:, None], (N_EXPERTS, LANES))
            comb_t, dest_t, cnt = _router(x2d, wr_hi, wr_lo, br, moe_tb)
            counts = cnt[:, :, :moe_tb // MOE_TOKEN_CHUNK + 1].reshape(-1)
            x2d = _moe_ffn_ln(x2d, counts, dest_t, comb_t, moe_w_gate[j].astype(BF16), moe_w_up[j].astype(BF16),
                              moe_w_down[j].astype(BF16), ln2_g[l][None, :], ln2_b[l][None, :], alpha, moe_tb)
    return x2d.reshape(bsz, s, d)
```

```python
import functools

import jax
import jax.numpy as jnp
import numpy as np
from jax import lax
from jax.experimental import pallas as pl
from jax.experimental.pallas import tpu as pltpu

F32 = jnp.float32
BF16 = jnp.bfloat16
I32 = jnp.int32

ML_HEADS = 4
ML_DIM = 128
ML_WIDTH = ML_HEADS * ML_DIM
CONV_W = 4
AT_HEADS = 8
AT_KV_HEADS = 2
AT_DIM = 64
AT_WIDTH = AT_HEADS * AT_DIM
KV_WIDTH = AT_KV_HEADS * AT_DIM
IDX_HEADS = 4
IDX_DIM = 64
TOPK_MAX = 256
ROPE_THETA = 500000.0
N_EXPERTS = 8
LN_EPS = 1e-5

LANES = 128
SUBLANES = 8
VMEM_LIMIT_BYTES = 56 * 1024 * 1024

COL_MQ = 0
COL_MK = 512
COL_MV = 1024
COL_MO = 1536
COL_AQ = 2048
COL_AKV = 2560
COL_IQ = 2816
COL_SMALL = 3072
PROJ_PAD = 3200
SM_IW = 64
SM_MI = 68
SM_MF = 72

MOE_TOKEN_CHUNK = 2 * LANES
MOE_ROW_TILE = LANES

NEG_BIG = -1e30
BF16_INF_PATTERN = 0x7F80
LOG2_E = 1.4426950408889634
VT_ROWS = AT_DIM + 2 * SUBLANES
NT_DIMS = (((1,), (1,)), ((), ()))
TN_DIMS = (((0,), (0,)), ((), ()))


def _cparams(n_axes):
    return pltpu.CompilerParams(dimension_semantics=("arbitrary",) * n_axes,
                                vmem_limit_bytes=VMEM_LIMIT_BYTES)


def _dot(a, b):
    return jnp.dot(a, b, preferred_element_type=F32)


def _layer_norm(z, g, b):
    mu = jnp.mean(z, axis=-1, keepdims=True)
    d = z - mu
    var = jnp.mean(d * d, axis=-1, keepdims=True)
    return d * lax.rsqrt(var + LN_EPS) * g + b


def _silu(x):
    return x / (1.0 + jnp.exp(-x))


def _in_proj_kernel(x_ref, w_ref, o_ref):
    o_ref[...] = _dot(x_ref[...].astype(BF16), w_ref[...])


def _in_proj(x2d, w_bf):
    n, d = x2d.shape
    pw = w_bf.shape[1]
    tm = 512
    return pl.pallas_call(
        _in_proj_kernel,
        grid=(n // tm,),
        in_specs=[pl.BlockSpec((tm, d), lambda i: (i, 0)),
                  pl.BlockSpec((d, pw), lambda i: (0, 0))],
        out_specs=pl.BlockSpec((tm, pw), lambda i: (i, 0)),
        out_shape=jax.ShapeDtypeStruct((n, pw), F32),
        compiler_params=_cparams(1),
        name="in_proj",
    )(x2d, w_bf)


def _mlstm_kernel(q_ref, k_ref, v_ref, o_ref, sm_ref, cw_ref, cb_ref, gb_ref, ng_ref, out_ref,
                  c_st, n_st, m_st, tail):
    @pl.when(pl.program_id(1) == 0)
    def _():
        c_st[...] = jnp.zeros_like(c_st)
        n_st[...] = jnp.zeros_like(n_st)
        m_st[...] = jnp.zeros_like(m_st)
        tail[...] = jnp.zeros_like(tail)

    for bb in range(q_ref.shape[0]):
        _mlstm_chunk(q_ref.at[bb], k_ref.at[bb], v_ref.at[bb], o_ref.at[bb], sm_ref.at[bb], cw_ref, cb_ref, gb_ref,
                     ng_ref, out_ref.at[bb], c_st.at[bb], n_st.at[bb], m_st.at[bb], tail.at[bb])


def _mlstm_chunk(q_ref, k_ref, v_ref, o_ref, sm_ref, cw_ref, cb_ref, gb_ref, ng_ref, out_ref,
                 c_st, n_st, m_st, tail):
    chunk = q_ref.shape[0]

    def conv_silu(x, prev, w, b):
        cat = jnp.concatenate([prev, x], axis=0)
        y = pltpu.roll(cat, 3, 0)[SUBLANES:] * w[0:1]
        y = y + pltpu.roll(cat, 2, 0)[SUBLANES:] * w[1:2]
        y = y + pltpu.roll(cat, 1, 0)[SUBLANES:] * w[2:3]
        y = y + x * w[3:4]
        return _silu(y + b)

    xq = q_ref[...]
    xk = k_ref[...]
    cw = cw_ref[...]
    cb = cb_ref[...]
    qs = conv_silu(xq, tail[:, :ML_WIDTH], cw[:, :ML_WIDTH], cb[:, :ML_WIDTH]) * (ML_DIM ** -0.5)
    ks = conv_silu(xk, tail[:, ML_WIDTH:], cw[:, ML_WIDTH:], cb[:, ML_WIDTH:])
    tail[:, :ML_WIDTH] = xq[chunk - SUBLANES:]
    tail[:, ML_WIDTH:] = xk[chunk - SUBLANES:]

    gates = sm_ref[...] + gb_ref[...]
    logf = jnp.minimum(gates, 0.0) - jnp.log1p(jnp.exp(-jnp.abs(gates)))
    row = lax.broadcasted_iota(I32, (chunk, chunk), 0)
    col = lax.broadcasted_iota(I32, (chunk, chunk), 1)
    causal = row >= col
    tri = jnp.where(causal, 1.0, 0.0).astype(BF16)
    hi = logf.astype(BF16)
    r1 = logf - hi.astype(F32)
    mid = r1.astype(BF16)
    lo = (r1 - mid.astype(F32)).astype(BF16)
    bcum = _dot(tri, hi) + _dot(tri, mid) + _dot(tri, lo)
    gates_t = gates.T
    bcum_t = bcum.T

    for h in range(ML_HEADS):
        sl = slice(h * ML_DIM, (h + 1) * ML_DIM)
        qh = qs[:, sl]
        kh = ks[:, sl]
        vb = v_ref[:, sl].astype(BF16)
        b = bcum[:, SM_MF + h:SM_MF + h + 1]
        li = gates[:, SM_MI + h:SM_MI + h + 1]
        r = gates_t[SM_MI + h:SM_MI + h + 1, :] - bcum_t[SM_MF + h:SM_MF + h + 1, :]
        m_prev = m_st[h][0:1, 0:1]
        dmat = jnp.where(causal, b + r, -jnp.inf)
        g_inter = b + m_prev
        m_t = jnp.maximum(g_inter, jnp.max(dmat, axis=1, keepdims=True))
        w_inter = jnp.exp(g_inter - m_t)
        qb = qh.astype(BF16)
        kb = kh.astype(BF16)
        sc = lax.dot_general(qb, kb, NT_DIMS, preferred_element_type=F32) * jnp.exp(dmat - m_t)
        c_old = c_st[h]
        n_old = n_st[h][0:1, :]
        num = w_inter * _dot(qb, c_old.astype(BF16)) + _dot(sc.astype(BF16), vb)
        den = w_inter * jnp.sum(qh * n_old, axis=1, keepdims=True) + jnp.sum(sc, axis=1, keepdims=True)
        hh = num / jnp.maximum(jnp.abs(den), jnp.exp(-m_t))
        b_last = b[chunk - 1:chunk, :]
        g_state = b_last + m_prev
        ls = b_last - b + li
        m_new = jnp.maximum(g_state, jnp.max(ls, axis=0, keepdims=True))
        ws = jnp.exp(ls - m_new)
        decay = jnp.exp(g_state - m_new)
        kw = kh * ws
        c_st[h] = decay * c_old + _dot(kw.T.astype(BF16), vb)
        n_st[h] = jnp.broadcast_to(decay * n_old + jnp.sum(kw, axis=0, keepdims=True), (SUBLANES, ML_DIM))
        m_st[h] = jnp.broadcast_to(m_new, (SUBLANES, LANES))
        mu = jnp.mean(hh, axis=1, keepdims=True)
        d = hh - mu
        var = jnp.mean(d * d, axis=1, keepdims=True)
        hn = d * lax.rsqrt(var + LN_EPS) * ng_ref[:, sl]
        out_ref[:, sl] = (hn / (1.0 + jnp.exp(-o_ref[:, sl]))).astype(out_ref.dtype)


def _mlstm(proj, conv_w, conv_b, gate_b, norm_g, chunk):
    bsz, s, _ = proj.shape
    wblk = ML_WIDTH
    nb = 1

    def colspec(col, width):
        return pl.BlockSpec((nb, chunk, width), lambda b, c: (b, c, col // width))

    def full2d(shape):
        return pl.BlockSpec(shape, lambda b, c: (0, 0))

    return pl.pallas_call(
        _mlstm_kernel,
        grid=(bsz // nb, s // chunk),
        in_specs=[colspec(COL_MQ, wblk), colspec(COL_MK, wblk), colspec(COL_MV, wblk), colspec(COL_MO, wblk),
                  colspec(COL_SMALL, LANES),
                  full2d((CONV_W, 2 * ML_WIDTH)), full2d((1, 2 * ML_WIDTH)), full2d((1, LANES)),
                  full2d((1, ML_WIDTH))],
        out_specs=pl.BlockSpec((nb, chunk, ML_WIDTH), lambda b, c: (b, c, 0)),
        out_shape=jax.ShapeDtypeStruct((bsz, s, ML_WIDTH), BF16),
        scratch_shapes=[pltpu.VMEM((nb, ML_HEADS, ML_DIM, ML_DIM), F32),
                        pltpu.VMEM((nb, ML_HEADS, SUBLANES, ML_DIM), F32),
                        pltpu.VMEM((nb, ML_HEADS, SUBLANES, LANES), F32),
                        pltpu.VMEM((nb, SUBLANES, 2 * ML_WIDTH), F32)],
        compiler_params=_cparams(2),
        name="mlstm",
    )(proj, proj, proj, proj, proj, conv_w, conv_b, gate_b, norm_g)


def _rope(x, cos, sinp, sinm):
    width = x.shape[1]
    reps = width // LANES
    if reps > 1:
        cos = jnp.concatenate([cos] * reps, axis=1)
        sinp = jnp.concatenate([sinp] * reps, axis=1)
        sinm = jnp.concatenate([sinm] * reps, axis=1)
    half = AT_DIM // 8
    return x * cos + pltpu.roll(x, half, 1) * sinp + pltpu.roll(x, width - half, 1) * sinm


def _dsa_prep_kernel(aq_ref, akv_ref, iq_ref, sm_ref, cos_ref, sinp_ref, sinm_ref, lng_ref, lnb_ref,
                     qt_out, k_out, vt_out, iqt_out, ik_out, iwt_out):
    cos = cos_ref[...]
    sinp = sinp_ref[...]
    sinm = sinm_ref[...]
    q = _rope(aq_ref[...], cos, sinp, sinm) * (AT_DIM ** -0.5 * LOG2_E)
    sm = sm_ref[...]
    lane = lax.broadcasted_iota(I32, sm.shape, 1)
    rep = AT_HEADS // AT_KV_HEADS
    for pair in range(AT_HEADS // 2):
        slab = q[:, pair * LANES:(pair + 1) * LANES]
        swapped = pltpu.roll(slab, AT_DIM, 1)
        for p in range(2):
            h = 2 * pair + p
            g = h // rep
            src = slab if p == g else swapped
            in_group = (lane >= g * AT_DIM) & (lane < (g + 1) * AT_DIM)
            qt_out[h * LANES:(h + 1) * LANES, :] = jnp.where(in_group, src, 0.0).T.astype(BF16)
    akv = akv_ref[...]
    k_out[...] = _rope(akv[:, :KV_WIDTH], cos, sinp, sinm).astype(BF16)
    v_t = akv[:, KV_WIDTH:].T
    t_len = v_t.shape[1]
    tail = jnp.where(lax.broadcasted_iota(I32, (VT_ROWS - AT_DIM, t_len), 0) == 0, 1.0, 0.0)
    for g in range(AT_KV_HEADS):
        vt_out[g] = jnp.concatenate([v_t[g * AT_DIM:(g + 1) * AT_DIM], tail], axis=0).astype(BF16)
    iqt_out[...] = (_rope(iq_ref[...], cos, sinp, sinm) * (IDX_DIM ** -0.5)).T.astype(BF16)
    is_k = lane < IDX_DIM
    mu = jnp.sum(jnp.where(is_k, sm, 0.0), axis=1, keepdims=True) * (1.0 / IDX_DIM)
    d = jnp.where(is_k, sm - mu, 0.0)
    var = jnp.sum(d * d, axis=1, keepdims=True) * (1.0 / IDX_DIM)
    y = jnp.where(is_k, d * lax.rsqrt(var + LN_EPS) * lng_ref[...] + lnb_ref[...], 0.0)
    ik_out[...] = _rope(y, cos, sinp, sinm).astype(BF16)
    iwt_out[...] = sm.T[SM_IW:SM_IW + SUBLANES, :] * (IDX_HEADS ** -0.5)


def _dsa_prep(proj, cos, sinp, sinm, ln_g, ln_b):
    bsz, s, _ = proj.shape
    t = 512

    def colspec(col, width):
        return pl.BlockSpec((None, t, width), lambda b, c: (b, c, col // width))

    tab = pl.BlockSpec((t, LANES), lambda b, c: (c, 0))
    vec = pl.BlockSpec((1, LANES), lambda b, c: (0, 0))

    def outspec(width):
        return pl.BlockSpec((None, t, width), lambda b, c: (b, c, 0))

    return pl.pallas_call(
        _dsa_prep_kernel,
        grid=(bsz, s // t),
        in_specs=[colspec(COL_AQ, AT_WIDTH), colspec(COL_AKV, 2 * KV_WIDTH), colspec(COL_IQ, IDX_HEADS * IDX_DIM),
                  colspec(COL_SMALL, LANES), tab, tab, tab, vec, vec],
        out_specs=[pl.BlockSpec((None, AT_HEADS * LANES, t), lambda b, c: (b, 0, c)),
                   outspec(KV_WIDTH),
                   pl.BlockSpec((None, AT_KV_HEADS, VT_ROWS, t), lambda b, c: (b, 0, 0, c)),
                   pl.BlockSpec((None, IDX_HEADS * IDX_DIM, t), lambda b, c: (b, 0, c)),
                   outspec(LANES),
                   pl.BlockSpec((None, SUBLANES, t), lambda b, c: (b, 0, c))],
        out_shape=[jax.ShapeDtypeStruct((bsz, AT_HEADS * LANES, s), BF16),
                   jax.ShapeDtypeStruct((bsz, s, KV_WIDTH), BF16),
                   jax.ShapeDtypeStruct((bsz, AT_KV_HEADS, VT_ROWS, s), BF16),
                   jax.ShapeDtypeStruct((bsz, IDX_HEADS * IDX_DIM, s), BF16),
                   jax.ShapeDtypeStruct((bsz, s, LANES), BF16),
                   jax.ShapeDtypeStruct((bsz, SUBLANES, s), F32)],
        compiler_params=_cparams(2),
        name="dsa_prep",
    )(proj, proj, proj, proj, cos, sinp, sinm, ln_g, ln_b)


def _tree_sum(parts):
    while len(parts) > 1:
        parts = [parts[i] + parts[i + 1] for i in range(0, len(parts) - 1, 2)] + ([parts[-1]] if len(parts) % 2 else [])
    return parts[0]


def _dsa_kernel(qt_ref, iqt_ref, iwt_ref, k_ref, vt_ref, ik_ref, out_ref, score, coarse, bias, acc, m_s, x_s,
                *, topk, seq, ck):
    tq = qt_ref.shape[1]
    j = pl.program_id(1)
    nck = (j * tq + tq + ck - 1) // ck
    t_abs = j * tq + lax.broadcasted_iota(I32, (ck, tq), 1)
    s_loc = lax.broadcasted_iota(I32, (ck, tq), 0)
    rep = AT_HEADS // AT_KV_HEADS
    pack = 2 * SUBLANES

    iqt = iqt_ref[...]
    iwt = iwt_ref[...]
    iq_all = jnp.concatenate([iqt[h * IDX_DIM:(h + 1) * IDX_DIM, :] for h in range(IDX_HEADS)], axis=1)
    iw_all = jnp.concatenate([iwt[h:h + 1, :] for h in range(IDX_HEADS)], axis=1)

    def chunk_off(c):
        return pl.multiple_of(c * ck, ck)

    def score_body(c, carry):
        off = chunk_off(c)
        kic = ik_ref[pl.ds(off, ck), :][:, :IDX_DIM]
        sw = jnp.maximum(_dot(kic, iq_all), 0.0) * iw_all
        s = sw[:, :tq]
        for h in range(1, IDX_HEADS):
            s = s + sw[:, h * tq:(h + 1) * tq]
        s = jnp.where(off + s_loc <= t_abs, s, -jnp.inf)
        s = jnp.where(s == 0.0, 0.0, s)
        score[pl.ds(off, ck), :] = s
        coarse[pl.ds(off, ck), :] = s.astype(BF16)
        return carry

    lax.fori_loop(0, nck, score_body, 0)

    def count(pred):
        def body(c, a):
            off = chunk_off(c)
            m = jnp.where(pred(score[pl.ds(off, ck), :], off), 1, 0)
            return a + jnp.sum(m.reshape(ck // SUBLANES, SUBLANES, tq), axis=0)
        a = lax.fori_loop(0, nck, body, jnp.zeros((SUBLANES, tq), I32))
        return jnp.sum(a, axis=0, keepdims=True)

    def count_coarse(pred):
        one = jnp.ones((), BF16)
        zero = jnp.zeros((), BF16)

        def body(c, a):
            m = jnp.where(pred(coarse[pl.ds(chunk_off(c), ck), :]), one, zero).reshape(ck // pack, pack, tq)
            return a + _tree_sum([m[i] for i in range(ck // pack)]).astype(F32)
        a = lax.fori_loop(0, nck, body, jnp.zeros((pack, tq), F32))
        return jnp.sum(a, axis=0, keepdims=True).astype(I32)

    def key_to_f32(k):
        return pltpu.bitcast(k ^ ((k >> 31) & jnp.int32(0x7FFFFFFF)), F32)

    def key16_to_bf16(k):
        pattern = (k ^ ((k >> 15) & 0x7FFF)) & 0xFFFF
        return pltpu.bitcast(pattern << 16, F32).astype(BF16)

    lo16 = -(1 << 15)
    t16 = jnp.full((1, tq), lo16, I32)
    for bit in reversed(range(16)):
        cand = t16 + (1 << bit)
        cand_b = key16_to_bf16(cand)
        c = count_coarse(lambda hc, cand_b=cand_b: hc >= cand_b)
        t16 = jnp.where(c >= topk, cand, t16)
    key_lo = jnp.clip(t16 - jnp.where(t16 >= 0, 1, 0), lo16, BF16_INF_PATTERN - 1) << 16
    offs = jnp.zeros((1, tq), I32)
    for bit in reversed(range(17)):
        cand = offs + (1 << bit)
        cand_f = key_to_f32(key_lo + cand)
        c = count(lambda sc, off, cand_f=cand_f: sc >= cand_f)
        offs = jnp.where(c >= topk, cand, offs)
    thr = key_to_f32(key_lo + offs)

    take_all = t_abs[0:1, :] < topk
    n_gt = count(lambda sc, off: sc > thr)
    n_eq = count(lambda sc, off: sc == thr)
    need = topk - n_gt
    excess = (n_eq > need) & jnp.logical_not(take_all)
    x_s[...] = jnp.full(x_s.shape, seq, I32)

    any_excess = jnp.max(jnp.where(excess, 1, 0)) > 0

    @pl.when(any_excess)
    def _():
        x = jnp.zeros((1, tq), I32)
        for bit in reversed(range(max(seq - 1, 1).bit_length())):
            cand = x + (1 << bit)
            c = count(lambda sc, off: (sc == thr) & (off + s_loc < cand))
            x = jnp.where(c < need, cand, x)
        x_s[...] = jnp.broadcast_to(jnp.where(excess, x, seq), x_s.shape)

    def write_bias(with_ties):
        xlim = x_s[0:1, :]

        def bias_body(c, carry):
            off = chunk_off(c)
            sc = score[pl.ds(off, ck), :]
            s_abs = off + s_loc
            if with_ties:
                sel = (sc > thr) | ((sc == thr) & (s_abs <= xlim)) | take_all
            else:
                sel = (sc >= thr) | take_all
            bias[pl.ds(off, ck), :] = jnp.where(sel & (s_abs <= t_abs), 0.0, NEG_BIG)
            return carry

        lax.fori_loop(0, nck, bias_body, 0)

    pl.when(any_excess)(functools.partial(write_bias, True))
    pl.when(jnp.logical_not(any_excess))(functools.partial(write_bias, False))

    m_s[...] = jnp.full(m_s.shape, NEG_BIG, F32)
    acc[...] = jnp.zeros_like(acc)

    def att_body(c, carry):
        off = chunk_off(c)
        kc = k_ref[pl.ds(off, ck), :]
        bc = bias[pl.ds(off, ck), :]
        b_all = jnp.concatenate([bc] * rep, axis=1)
        for g in range(AT_KV_HEADS):
            qg = jnp.concatenate([qt_ref[(g * rep + r) * LANES:(g * rep + r + 1) * LANES, :] for r in range(rep)],
                                 axis=1)
            lg = _dot(kc, qg) + b_all
            m_old = m_s[g]
            m_new = jnp.maximum(m_old, jnp.max(lg, axis=0, keepdims=True))
            a = jnp.exp2(m_old - m_new)
            p = jnp.exp2(lg - m_new)
            acc[g] = a * acc[g] + _dot(vt_ref[g, :, pl.ds(off, ck)], p.astype(BF16))
            m_s[g] = m_new
        return carry

    lax.fori_loop(0, nck, att_body, 0)

    for g in range(AT_KV_HEADS):
        ag = acc[g]
        on = ag[:AT_DIM] / ag[AT_DIM:AT_DIM + 1]
        for pp in range(rep // 2):
            two = jnp.concatenate([on[:, 2 * pp * tq:(2 * pp + 1) * tq], on[:, (2 * pp + 1) * tq:(2 * pp + 2) * tq]],
                                  axis=0)
            pair = g * (rep // 2) + pp
            out_ref[:, pair * LANES:(pair + 1) * LANES] = two.T.astype(out_ref.dtype)


def _dsa_attention(q_t, k_r, v_t, iq_t, ik_r, iw_t):
    bsz, s, _ = k_r.shape
    tq = min(512, s)
    topk = min(TOPK_MAX, s // 4)
    rep = AT_HEADS // AT_KV_HEADS
    ck = min(512, s)
    assert s % ck == 0 and ck >= topk and ck % tq == 0
    kern = functools.partial(_dsa_kernel, topk=topk, seq=s, ck=ck)
    return pl.pallas_call(
        kern,
        grid=(bsz, s // tq),
        in_specs=[pl.BlockSpec((None, AT_HEADS * LANES, tq), lambda b, j: (b, 0, j)),
                  pl.BlockSpec((None, IDX_HEADS * IDX_DIM, tq), lambda b, j: (b, 0, j)),
                  pl.BlockSpec((None, SUBLANES, tq), lambda b, j: (b, 0, j)),
                  pl.BlockSpec((None, s, KV_WIDTH), lambda b, j: (b, 0, 0)),
                  pl.BlockSpec((None, AT_KV_HEADS, VT_ROWS, s), lambda b, j: (b, 0, 0, 0)),
                  pl.BlockSpec((None, s, LANES), lambda b, j: (b, 0, 0))],
        out_specs=pl.BlockSpec((None, tq, AT_WIDTH), lambda b, j: (b, j, 0)),
        out_shape=jax.ShapeDtypeStruct((bsz, s, AT_WIDTH), BF16),
        scratch_shapes=[pltpu.VMEM((s, tq), F32),
                        pltpu.VMEM((s, tq), BF16),
                        pltpu.VMEM((s, tq), F32),
                        pltpu.VMEM((AT_KV_HEADS, VT_ROWS, rep * tq), F32),
                        pltpu.VMEM((AT_KV_HEADS, 1, rep * tq), F32),
                        pltpu.VMEM((SUBLANES, tq), I32)],
        compiler_params=_cparams(2),
        name="dsa_attn",
    )(q_t, iq_t, iw_t, k_r, v_t, ik_r)


def _out_ln_kernel(ml_ref, at_ref, x_ref, wt_ref, wb_ref, g_ref, b_ref, o_ref, *, alpha):
    mix = _dot(ml_ref[...], wt_ref[...]) + _dot(at_ref[...], wb_ref[...])
    o_ref[...] = _layer_norm(alpha * x_ref[...] + mix, g_ref[...], b_ref[...])


def _out_proj_ln(ml, at, x2d, w_top, w_bot, g, b, alpha):
    n, d = x2d.shape
    tm = 512
    row = lambda i: (i, 0)
    const = lambda i: (0, 0)
    return pl.pallas_call(
        functools.partial(_out_ln_kernel, alpha=alpha),
        grid=(n // tm,),
        in_specs=[pl.BlockSpec((tm, ML_WIDTH), row), pl.BlockSpec((tm, AT_WIDTH), row), pl.BlockSpec((tm, d), row),
                  pl.BlockSpec((ML_WIDTH, d), const), pl.BlockSpec((AT_WIDTH, d), const),
                  pl.BlockSpec((1, d), const), pl.BlockSpec((1, d), const)],
        out_specs=pl.BlockSpec((tm, d), row),
        out_shape=jax.ShapeDtypeStruct((n, d), F32),
        compiler_params=_cparams(1),
        name="out_proj_ln",
    )(ml, at, x2d, w_top, w_bot, g, b)


def _ffn_kernel(x_ref, wg_ref, wu_ref, wd_ref, g_ref, b_ref, o_ref, acc_ref, xb_ref, *, alpha):
    f = pl.program_id(1)

    @pl.when(f == 0)
    def _():
        xb_ref[...] = x_ref[...].astype(BF16)
        acc_ref[...] = jnp.zeros_like(acc_ref)

    xb = xb_ref[...]
    h = _silu(_dot(xb, wg_ref[...])) * _dot(xb, wu_ref[...])
    acc_ref[...] += _dot(h.astype(BF16), wd_ref[...])

    @pl.when(f == pl.num_programs(1) - 1)
    def _():
        o_ref[...] = _layer_norm(alpha * x_ref[...] + acc_ref[...], g_ref[...], b_ref[...])


def _ffn_chunk(d_ff):
    best = LANES
    for c in range(LANES, 1408 + 1, LANES):
        if d_ff % c == 0:
            best = c
    return best


def _dense_ffn_ln(x2d, wg, wu, wd, g, b, alpha):
    n, d = x2d.shape
    d_ff = wg.shape[1]
    tm = 512
    fc = _ffn_chunk(d_ff)
    return pl.pallas_call(
        functools.partial(_ffn_kernel, alpha=alpha),
        grid=(n // tm, d_ff // fc),
        in_specs=[pl.BlockSpec((tm, d), lambda i, f: (i, 0)),
                  pl.BlockSpec((d, fc), lambda i, f: (0, f)),
                  pl.BlockSpec((d, fc), lambda i, f: (0, f)),
                  pl.BlockSpec((fc, d), lambda i, f: (f, 0)),
                  pl.BlockSpec((1, d), lambda i, f: (0, 0)),
                  pl.BlockSpec((1, d), lambda i, f: (0, 0))],
        out_specs=pl.BlockSpec((tm, d), lambda i, f: (i, 0)),
        out_shape=jax.ShapeDtypeStruct((n, d), F32),
        scratch_shapes=[pltpu.VMEM((tm, d), F32), pltpu.VMEM((tm, d), BF16)],
        compiler_params=_cparams(2),
        name="dense_ffn_ln",
    )(x2d, wg, wu, wd, g, b)


def _router_kernel(x_ref, wh_ref, wl_ref, br_ref, comb_ref, dest_ref, cnt_ref):
    tb = x_ref.shape[0]
    x = x_ref[...]
    xh = x.astype(BF16)
    xl = (x - xh.astype(F32)).astype(BF16)
    wh = wh_ref[...]
    lg = (lax.dot_general(wh, xh, NT_DIMS, preferred_element_type=F32)
          + lax.dot_general(wh, xl, NT_DIMS, preferred_element_type=F32)
          + lax.dot_general(wl_ref[...], xh, NT_DIMS, preferred_element_type=F32)) + br_ref[:, 0:1]
    e_id = lax.broadcasted_iota(I32, (N_EXPERTS, tb), 0)
    m1 = jnp.max(lg, axis=0, keepdims=True)
    i1 = jnp.min(jnp.where(lg == m1, e_id, N_EXPERTS), axis=0, keepdims=True)
    lg2 = jnp.where(e_id == i1, -jnp.inf, lg)
    m2 = jnp.max(lg2, axis=0, keepdims=True)
    i2 = jnp.min(jnp.where(lg2 == m2, e_id, N_EXPERTS), axis=0, keepdims=True)
    ex = jnp.exp(m2 - m1)
    g1 = 1.0 / (1.0 + ex)
    g2 = ex / (1.0 + ex)
    comb = jnp.where(e_id == i1, g1, 0.0) + jnp.where(e_id == i2, g2, 0.0)
    comb_ref[...] = comb
    mask = jnp.where((e_id == i1) | (e_id == i2), 1.0, 0.0)
    w = MOE_TOKEN_CHUNK
    upper = jnp.where(lax.broadcasted_iota(I32, (w, w), 0) < lax.broadcasted_iota(I32, (w, w), 1), 1.0, 0.0).astype(BF16)
    lane = lax.broadcasted_iota(I32, (N_EXPERTS, LANES), 1)
    cnt = jnp.zeros((N_EXPERTS, LANES), F32)
    off = jnp.zeros((N_EXPERTS, 1), F32)
    for c in range(tb // w):
        cnt = jnp.where(lane == c, off, cnt)
        mc = mask[:, c * w:(c + 1) * w]
        dest_ref[:, c * w:(c + 1) * w] = (_dot(mc.astype(BF16), upper) + off).astype(I32)
        off = off + jnp.sum(mc, axis=1, keepdims=True)
    cnt_ref[...] = jnp.where(lane == tb // w, off, cnt).astype(I32)


def _router(x2d, w_hi, w_lo, b_r, tb):
    n, d = x2d.shape
    nb = n // tb
    return pl.pallas_call(
        _router_kernel,
        grid=(nb,),
        in_specs=[pl.BlockSpec((tb, d), lambda i: (i, 0)),
                  pl.BlockSpec((N_EXPERTS, d), lambda i: (0, 0)),
                  pl.BlockSpec((N_EXPERTS, d), lambda i: (0, 0)),
                  pl.BlockSpec((N_EXPERTS, LANES), lambda i: (0, 0))],
        out_specs=[pl.BlockSpec((N_EXPERTS, tb), lambda i: (0, i)),
                   pl.BlockSpec((N_EXPERTS, tb), lambda i: (0, i)),
                   pl.BlockSpec((None, N_EXPERTS, LANES), lambda i: (i, 0, 0))],
        out_shape=[jax.ShapeDtypeStruct((N_EXPERTS, n), F32),
                   jax.ShapeDtypeStruct((N_EXPERTS, n), I32),
                   jax.ShapeDtypeStruct((nb, N_EXPERTS, LANES), I32)],
        compiler_params=_cparams(1),
        name="moe_router",
    )(x2d, w_hi, w_lo, b_r)


def _moe_kernel(cnt_ref, x_ref, dest_ref, comb_ref, wg_ref, wu_ref, wd_ref, g_ref, b_ref, o_ref,
                xb, xe, ye, gs, *, alpha):
    sub = MOE_ROW_TILE
    cw = MOE_TOKEN_CHUNK
    tb, d = x_ref.shape
    nch = tb // cw
    i = pl.program_id(0)
    e = pl.program_id(1)
    f = pl.program_id(2)
    n_e = pl.num_programs(1)
    n_f = pl.num_programs(2)
    base = (i * n_e + e) * (nch + 1)
    cum = [cnt_ref[base + k] for k in range(nch + 1)]
    nsub = (cum[nch] + sub - 1) // sub

    @pl.when((e == 0) & (f == 0))
    def _():
        xb[...] = x_ref[...].astype(BF16)
        o_ref[...] = jnp.zeros_like(o_ref)

    def onehot(r0, ch):
        cols = slice(ch * cw, (ch + 1) * cw)
        drow = dest_ref[pl.ds(e, 1), cols]
        crow = comb_ref[pl.ds(e, 1), cols]
        slot = r0 + lax.broadcasted_iota(I32, (sub, cw), 0)
        return jnp.where((drow == slot) & (crow > 0.0), 1.0, 0.0).astype(BF16)

    def for_overlapping_chunks(r0, fn):
        for ch in range(nch):
            pl.when((cum[ch] < r0 + sub) & (cum[ch + 1] > r0))(functools.partial(fn, ch))

    @pl.when(f == 0)
    def _():
        def body(r, carry):
            r0 = pl.multiple_of(r * sub, sub)
            rows = pl.ds(r0, sub)
            xe[rows, :] = jnp.zeros((sub, d), BF16)
            ye[rows, :] = jnp.zeros((sub, d), F32)
            gs[rows, :] = jnp.zeros((sub, LANES), F32)

            def gather(ch):
                cols = slice(ch * cw, (ch + 1) * cw)
                p = onehot(r0, ch)
                xe[rows, :] += _dot(p, xb[cols, :]).astype(BF16)
                crow = comb_ref[pl.ds(e, 1), cols]
                hi = crow.astype(BF16)
                lo = (crow - hi.astype(F32)).astype(BF16)
                c2 = jnp.concatenate([hi, lo, jnp.zeros((LANES - 2, cw), BF16)], axis=0)
                gs[rows, :] += lax.dot_general(p, c2, NT_DIMS, preferred_element_type=F32)

            for_overlapping_chunks(r0, gather)
            return carry
        lax.fori_loop(0, nsub, body, 0)

    def ffn_tile(r0, m):
        rows = pl.ds(r0, m)
        xr = xe[rows, :]
        h = _silu(_dot(xr, wg_ref[...])) * _dot(xr, wu_ref[...])
        ye[rows, :] += _dot(h.astype(BF16), wd_ref[...])

    def ffn_quad(r, carry):
        ffn_tile(pl.multiple_of(r * 4 * sub, 4 * sub), 4 * sub)
        return carry

    n_quad = nsub // 4
    lax.fori_loop(0, n_quad, ffn_quad, 0)
    rest = nsub - 4 * n_quad
    rest0 = n_quad * 4 * sub

    @pl.when(rest >= 2)
    def _():
        ffn_tile(pl.multiple_of(rest0, 2 * sub), 2 * sub)

    @pl.when(rest % 2 == 1)
    def _():
        ffn_tile(pl.multiple_of(rest0 + (rest // 2) * 2 * sub, sub), sub)

    @pl.when(f == n_f - 1)
    def _():
        def body(r, carry):
            r0 = pl.multiple_of(r * sub, sub)
            rows = pl.ds(r0, sub)
            gate = gs[rows, 0:1] + gs[rows, 1:2]
            yw = (ye[rows, :] * gate).astype(BF16)

            def scatter(ch):
                cols = slice(ch * cw, (ch + 1) * cw)
                o_ref[cols, :] += lax.dot_general(onehot(r0, ch), yw, TN_DIMS, preferred_element_type=F32)

            for_overlapping_chunks(r0, scatter)
            return carry
        lax.fori_loop(0, nsub, body, 0)

    @pl.when((e == n_e - 1) & (f == n_f - 1))
    def _():
        o_ref[...] = _layer_norm(alpha * x_ref[...] + o_ref[...], g_ref[...], b_ref[...])


def _moe_ffn_ln(x2d, counts, dest_t, comb_t, wg, wu, wd, g, b, alpha, tb):
    n, d = x2d.shape
    n_e, _, d_ff = wg.shape
    fc = 512
    grid_spec = pltpu.PrefetchScalarGridSpec(
        num_scalar_prefetch=1,
        grid=(n // tb, n_e, d_ff // fc),
        in_specs=[pl.BlockSpec((tb, d), lambda i, e, f, cnt: (i, 0), pipeline_mode=pl.Buffered(1)),
                  pl.BlockSpec((n_e, tb), lambda i, e, f, cnt: (0, i)),
                  pl.BlockSpec((n_e, tb), lambda i, e, f, cnt: (0, i)),
                  pl.BlockSpec((None, d, fc), lambda i, e, f, cnt: (e, 0, f)),
                  pl.BlockSpec((None, d, fc), lambda i, e, f, cnt: (e, 0, f)),
                  pl.BlockSpec((None, fc, d), lambda i, e, f, cnt: (e, f, 0)),
                  pl.BlockSpec((1, d), lambda i, e, f, cnt: (0, 0)),
                  pl.BlockSpec((1, d), lambda i, e, f, cnt: (0, 0))],
        out_specs=pl.BlockSpec((tb, d), lambda i, e, f, cnt: (i, 0)),
        scratch_shapes=[pltpu.VMEM((tb, d), BF16), pltpu.VMEM((tb, d), BF16), pltpu.VMEM((tb, d), F32),
                        pltpu.VMEM((tb, LANES), F32)],
    )
    return pl.pallas_call(
        functools.partial(_moe_kernel, alpha=alpha),
        grid_spec=grid_spec,
        out_shape=jax.ShapeDtypeStruct((n, d), F32),
        compiler_params=_cparams(3),
        name="moe_ffn_ln",
    )(counts, x2d, dest_t, comb_t, wg, wu, wd, g, b)


def _rope_tables(s):
    half = AT_DIM // 8
    inv = ROPE_THETA ** (-jnp.arange(half, dtype=F32) / half)
    ang = jnp.arange(s).astype(F32)[:, None] * inv[None, :]
    cos = jnp.cos(ang)
    sin = jnp.sin(ang)
    ones = jnp.ones((s, AT_DIM - 2 * half), F32)
    zeros = jnp.zeros((s, AT_DIM - 2 * half), F32)
    z8 = jnp.zeros((s, half), F32)
    cos64 = jnp.concatenate([cos, cos, ones], axis=1)
    sinp64 = jnp.concatenate([z8, sin, zeros], axis=1)
    sinm64 = jnp.concatenate([-sin, z8, zeros], axis=1)
    rep = LANES // AT_DIM
    return (jnp.tile(cos64, (1, rep)), jnp.tile(sinp64, (1, rep)), jnp.tile(sinm64, (1, rep)))


def _permute_w_in(w):
    n_in = w.shape[1]
    n_ml = 4 * ML_WIDTH
    n_gate = 2 * ML_HEADS
    src = np.full((PROJ_PAD,), -1, np.int32)
    src[:n_ml] = np.arange(n_ml)
    src[n_ml:n_in - n_gate] = np.arange(n_ml + n_gate, n_in)
    src[n_in - n_gate:n_in] = np.arange(n_ml, n_ml + n_gate)
    sel = (jnp.arange(n_in, dtype=I32)[:, None] == jnp.asarray(src)[None, :]).astype(BF16)
    return jnp.dot(w.astype(BF16), sel, preferred_element_type=F32).astype(BF16)


def kernel(x, w_in, ml_conv_w, ml_conv_b, ml_i_b, ml_f_b, ml_norm_g, idx_k_norm_g, idx_k_norm_b, w_out, ln1_g, ln1_b, ln2_g, ln2_b, ffn_w_gate, ffn_w_up, ffn_w_down, moe_w_router, moe_b_router, moe_w_gate, moe_w_up, moe_w_down):
    bsz, s, d = x.shape
    depth = w_in.shape[0]
    alpha = float((2 * depth) ** 0.25)
    n = bsz * s
    cos, sinp, sinm = _rope_tables(s)
    ml_chunk = min(256, s)
    moe_tb = min(2048, n)
    zpad = lambda v, left: jnp.pad(v, (left, LANES - left - v.shape[0]))[None, :]

    x2d = x.reshape(n, d)
    for l in range(depth):
        proj = _in_proj(x2d, _permute_w_in(w_in[l]).astype(BF16)).reshape(bsz, s, PROJ_PAD)
        gate_b = zpad(jnp.concatenate([ml_i_b[l], ml_f_b[l]]), SM_MI)
        ml_out = _mlstm(proj, ml_conv_w[l], ml_conv_b[l][None, :], gate_b, ml_norm_g[l][None, :], ml_chunk)
        q_r, k_r, v_t, iq_r, ik_r, iw_t = _dsa_prep(proj, cos, sinp, sinm,
                                                    zpad(idx_k_norm_g[l], 0), zpad(idx_k_norm_b[l], 0))
        at_out = _dsa_attention(q_r, k_r, v_t, iq_r, ik_r, iw_t)
        wo = w_out[l].astype(BF16)
        x2d = _out_proj_ln(ml_out.reshape(n, ML_WIDTH), at_out.reshape(n, AT_WIDTH), x2d,
                           wo[:ML_WIDTH], wo[ML_WIDTH:], ln1_g[l][None, :], ln1_b[l][None, :], alpha)
        j = l // 2
        if l % 2 == 0:
            x2d = _dense_ffn_ln(x2d, ffn_w_gate[j].astype(BF16), ffn_w_up[j].astype(BF16),
                                ffn_w_down[j].astype(BF16), ln2_g[l][None, :], ln2_b[l][None, :], alpha)
        else:
            wr_t = moe_w_router[j].T
            wr_hi = wr_t.astype(BF16)
            wr_lo = (wr_t - wr_hi.astype(F32)).astype(BF16)
            br = jnp.broadcast_to(moe_b_router[j][:, None], (N_EXPERTS, LANES))
            comb_t, dest_t, cnt = _router(x2d, wr_hi, wr_lo, br, moe_tb)
            counts = cnt[:, :, :moe_tb // MOE_TOKEN_CHUNK + 1].reshape(-1)
            x2d = _moe_ffn_ln(x2d, counts, dest_t, comb_t, moe_w_gate[j].astype(BF16), moe_w_up[j].astype(BF16),
                              moe_w_down[j].astype(BF16), ln2_g[l][None, :], ln2_b[l][None, :], alpha, moe_tb)
    return x2d.reshape(bsz, s, d)
```

```python
import functools

import jax
import jax.numpy as jnp
import numpy as np
from jax import lax
from jax.experimental import pallas as pl
from jax.experimental.pallas import tpu as pltpu

F32 = jnp.float32
BF16 = jnp.bfloat16
I32 = jnp.int32

ML_HEADS = 4
ML_DIM = 128
ML_WIDTH = ML_HEADS * ML_DIM
CONV_W = 4
AT_HEADS = 8
AT_KV_HEADS = 2
AT_DIM = 64
AT_WIDTH = AT_HEADS * AT_DIM
KV_WIDTH = AT_KV_HEADS * AT_DIM
IDX_HEADS = 4
IDX_DIM = 64
TOPK_MAX = 256
ROPE_THETA = 500000.0
N_EXPERTS = 8
LN_EPS = 1e-5

LANES = 128
SUBLANES = 8
VMEM_LIMIT_BYTES = 56 * 1024 * 1024

COL_MQ = 0
COL_MK = 512
COL_MV = 1024
COL_MO = 1536
COL_AQ = 2048
COL_AKV = 2560
COL_IQ = 2816
COL_SMALL = 3072
PROJ_PAD = 3200
SM_IW = 64
SM_MI = 68
SM_MF = 72

MOE_TOKEN_CHUNK = 2 * LANES
MOE_ROW_TILE = LANES

NEG_BIG = -1e30
BF16_INF_PATTERN = 0x7F80
LOG2_E = 1.4426950408889634
VT_ROWS = AT_DIM + 2 * SUBLANES
NT_DIMS = (((1,), (1,)), ((), ()))
TN_DIMS = (((0,), (0,)), ((), ()))


def _cparams(n_axes):
    return pltpu.CompilerParams(dimension_semantics=("arbitrary",) * n_axes,
                                vmem_limit_bytes=VMEM_LIMIT_BYTES)


def _dot(a, b):
    return jnp.dot(a, b, preferred_element_type=F32)


def _layer_norm(z, g, b):
    mu = jnp.mean(z, axis=-1, keepdims=True)
    d = z - mu
    var = jnp.mean(d * d, axis=-1, keepdims=True)
    return d * lax.rsqrt(var + LN_EPS) * g + b


def _silu(x):
    return x / (1.0 + jnp.exp(-x))


def _in_proj_kernel(x_ref, w_ref, o_ref):
    o_ref[...] = _dot(x_ref[...].astype(BF16), w_ref[...])


def _in_proj(x2d, w_bf):
    n, d = x2d.shape
    pw = w_bf.shape[1]
    tm = 512
    return pl.pallas_call(
        _in_proj_kernel,
        grid=(n // tm,),
        in_specs=[pl.BlockSpec((tm, d), lambda i: (i, 0)),
                  pl.BlockSpec((d, pw), lambda i: (0, 0))],
        out_specs=pl.BlockSpec((tm, pw), lambda i: (i, 0)),
        out_shape=jax.ShapeDtypeStruct((n, pw), F32),
        compiler_params=_cparams(1),
        name="in_proj",
    )(x2d, w_bf)


def _mlstm_kernel(q_ref, k_ref, v_ref, o_ref, sm_ref, cw_ref, cb_ref, gb_ref, ng_ref, out_ref,
                  c_st, n_st, m_st, tail):
    @pl.when(pl.program_id(1) == 0)
    def _():
        c_st[...] = jnp.zeros_like(c_st)
        n_st[...] = jnp.zeros_like(n_st)
        m_st[...] = jnp.zeros_like(m_st)
        tail[...] = jnp.zeros_like(tail)

    for bb in range(q_ref.shape[0]):
        _mlstm_chunk(q_ref.at[bb], k_ref.at[bb], v_ref.at[bb], o_ref.at[bb], sm_ref.at[bb], cw_ref, cb_ref, gb_ref,
                     ng_ref, out_ref.at[bb], c_st.at[bb], n_st.at[bb], m_st.at[bb], tail.at[bb])


def _mlstm_chunk(q_ref, k_ref, v_ref, o_ref, sm_ref, cw_ref, cb_ref, gb_ref, ng_ref, out_ref,
                 c_st, n_st, m_st, tail):
    chunk = q_ref.shape[0]

    def conv_silu(x, prev, w, b):
        cat = jnp.concatenate([prev, x], axis=0)
        y = pltpu.roll(cat, 3, 0)[SUBLANES:] * w[0:1]
        y = y + pltpu.roll(cat, 2, 0)[SUBLANES:] * w[1:2]
        y = y + pltpu.roll(cat, 1, 0)[SUBLANES:] * w[2:3]
        y = y + x * w[3:4]
        return _silu(y + b)

    xq = q_ref[...]
    xk = k_ref[...]
    cw = cw_ref[...]
    cb = cb_ref[...]
    qs = conv_silu(xq, tail[:, :ML_WIDTH], cw[:, :ML_WIDTH], cb[:, :ML_WIDTH]) * (ML_DIM ** -0.5)
    ks = conv_silu(xk, tail[:, ML_WIDTH:], cw[:, ML_WIDTH:], cb[:, ML_WIDTH:])
    tail[:, :ML_WIDTH] = xq[chunk - SUBLANES:]
    tail[:, ML_WIDTH:] = xk[chunk - SUBLANES:]

    gates = sm_ref[...] + gb_ref[...]
    logf = jnp.minimum(gates, 0.0) - jnp.log1p(jnp.exp(-jnp.abs(gates)))
    row = lax.broadcasted_iota(I32, (chunk, chunk), 0)
    col = lax.broadcasted_iota(I32, (chunk, chunk), 1)
    causal = row >= col
    tri = jnp.where(causal, 1.0, 0.0).astype(BF16)
    hi = logf.astype(BF16)
    r1 = logf - hi.astype(F32)
    mid = r1.astype(BF16)
    lo = (r1 - mid.astype(F32)).astype(BF16)
    bcum = _dot(tri, hi) + _dot(tri, mid) + _dot(tri, lo)
    gates_t = gates.T
    bcum_t = bcum.T

    for h in range(ML_HEADS):
        sl = slice(h * ML_DIM, (h + 1) * ML_DIM)
        qh = qs[:, sl]
        kh = ks[:, sl]
        vb = v_ref[:, sl].astype(BF16)
        b = bcum[:, SM_MF + h:SM_MF + h + 1]
        li = gates[:, SM_MI + h:SM_MI + h + 1]
        r = gates_t[SM_MI + h:SM_MI + h + 1, :] - bcum_t[SM_MF + h:SM_MF + h + 1, :]
        m_prev = m_st[h][0:1, 0:1]
        dmat = jnp.where(causal, b + r, -jnp.inf)
        g_inter = b + m_prev
        m_t = jnp.maximum(g_inter, jnp.max(dmat, axis=1, keepdims=True))
        w_inter = jnp.exp(g_inter - m_t)
        qb = qh.astype(BF16)
        kb = kh.astype(BF16)
        sc = lax.dot_general(qb, kb, NT_DIMS, preferred_element_type=F32) * jnp.exp(dmat - m_t)
        c_old = c_st[h]
        n_old = n_st[h][0:1, :]
        num = w_inter * _dot(qb, c_old.astype(BF16)) + _dot(sc.astype(BF16), vb)
        den = w_inter * jnp.sum(qh * n_old, axis=1, keepdims=True) + jnp.sum(sc, axis=1, keepdims=True)
        hh = num / jnp.maximum(jnp.abs(den), jnp.exp(-m_t))
        b_last = b[chunk - 1:chunk, :]
        g_state = b_last + m_prev
        ls = b_last - b + li
        m_new = jnp.maximum(g_state, jnp.max(ls, axis=0, keepdims=True))
        ws = jnp.exp(ls - m_new)
        decay = jnp.exp(g_state - m_new)
        kw = kh * ws
        c_st[h] = decay * c_old + _dot(kw.T.astype(BF16), vb)
        n_st[h] = jnp.broadcast_to(decay * n_old + jnp.sum(kw, axis=0, keepdims=True), (SUBLANES, ML_DIM))
        m_st[h] = jnp.broadcast_to(m_new, (SUBLANES, LANES))
        mu = jnp.mean(hh, axis=1, keepdims=True)
        d = hh - mu
        var = jnp.mean(d * d, axis=1, keepdims=True)
        hn = d * lax.rsqrt(var + LN_EPS) * ng_ref[:, sl]
        out_ref[:, sl] = (hn / (1.0 + jnp.exp(-o_ref[:, sl]))).astype(out_ref.dtype)


def _mlstm(proj, conv_w, conv_b, gate_b, norm_g, chunk):
    bsz, s, _ = proj.shape
    wblk = ML_WIDTH
    nb = 1

    def colspec(col, width):
        return pl.BlockSpec((nb, chunk, width), lambda b, c: (b, c, col // width))

    def full2d(shape):
        return pl.BlockSpec(shape, lambda b, c: (0, 0))

    return pl.pallas_call(
        _mlstm_kernel,
        grid=(bsz // nb, s // chunk),
        in_specs=[colspec(COL_MQ, wblk), colspec(COL_MK, wblk), colspec(COL_MV, wblk), colspec(COL_MO, wblk),
                  colspec(COL_SMALL, LANES),
                  full2d((CONV_W, 2 * ML_WIDTH)), full2d((1, 2 * ML_WIDTH)), full2d((1, LANES)),
                  full2d((1, ML_WIDTH))],
        out_specs=pl.BlockSpec((nb, chunk, ML_WIDTH), lambda b, c: (b, c, 0)),
        out_shape=jax.ShapeDtypeStruct((bsz, s, ML_WIDTH), BF16),
        scratch_shapes=[pltpu.VMEM((nb, ML_HEADS, ML_DIM, ML_DIM), F32),
                        pltpu.VMEM((nb, ML_HEADS, SUBLANES, ML_DIM), F32),
                        pltpu.VMEM((nb, ML_HEADS, SUBLANES, LANES), F32),
                        pltpu.VMEM((nb, SUBLANES, 2 * ML_WIDTH), F32)],
        compiler_params=_cparams(2),
        name="mlstm",
    )(proj, proj, proj, proj, proj, conv_w, conv_b, gate_b, norm_g)


def _rope(x, cos, sinp, sinm):
    width = x.shape[1]
    reps = width // LANES
    if reps > 1:
        cos = jnp.concatenate([cos] * reps, axis=1)
        sinp = jnp.concatenate([sinp] * reps, axis=1)
        sinm = jnp.concatenate([sinm] * reps, axis=1)
    half = AT_DIM // 8
    return x * cos + pltpu.roll(x, half, 1) * sinp + pltpu.roll(x, width - half, 1) * sinm


def _dsa_prep_kernel(aq_ref, akv_ref, iq_ref, sm_ref, cos_ref, sinp_ref, sinm_ref, lng_ref, lnb_ref,
                     qt_out, k_out, vt_out, iqt_out, ik_out, iwt_out):
    cos = cos_ref[...]
    sinp = sinp_ref[...]
    sinm = sinm_ref[...]
    q = _rope(aq_ref[...], cos, sinp, sinm) * (AT_DIM ** -0.5 * LOG2_E)
    sm = sm_ref[...]
    lane = lax.broadcasted_iota(I32, sm.shape, 1)
    rep = AT_HEADS // AT_KV_HEADS
    for pair in range(AT_HEADS // 2):
        slab = q[:, pair * LANES:(pair + 1) * LANES]
        swapped = pltpu.roll(slab, AT_DIM, 1)
        for p in range(2):
            h = 2 * pair + p
            g = h // rep
            src = slab if p == g else swapped
            in_group = (lane >= g * AT_DIM) & (lane < (g + 1) * AT_DIM)
            qt_out[h * LANES:(h + 1) * LANES, :] = jnp.where(in_group, src, 0.0).T.astype(BF16)
    akv = akv_ref[...]
    k_out[...] = _rope(akv[:, :KV_WIDTH], cos, sinp, sinm).astype(BF16)
    v_t = akv[:, KV_WIDTH:].T
    t_len = v_t.shape[1]
    tail = jnp.where(lax.broadcasted_iota(I32, (VT_ROWS - AT_DIM, t_len), 0) == 0, 1.0, 0.0)
    for g in range(AT_KV_HEADS):
        vt_out[g] = jnp.concatenate([v_t[g * AT_DIM:(g + 1) * AT_DIM], tail], axis=0).astype(BF16)
    iqt_out[...] = (_rope(iq_ref[...], cos, sinp, sinm) * (IDX_DIM ** -0.5)).T.astype(BF16)
    is_k = lane < IDX_DIM
    mu = jnp.sum(jnp.where(is_k, sm, 0.0), axis=1, keepdims=True) * (1.0 / IDX_DIM)
    d = jnp.where(is_k, sm - mu, 0.0)
    var = jnp.sum(d * d, axis=1, keepdims=True) * (1.0 / IDX_DIM)
    y = jnp.where(is_k, d * lax.rsqrt(var + LN_EPS) * lng_ref[...] + lnb_ref[...], 0.0)
    ik_out[...] = _rope(y, cos, sinp, sinm).astype(BF16)
    iwt_out[...] = sm.T[SM_IW:SM_IW + SUBLANES, :] * (IDX_HEADS ** -0.5)


def _dsa_prep(proj, cos, sinp, sinm, ln_g, ln_b):
    bsz, s, _ = proj.shape
    t = 512

    def colspec(col, width):
        return pl.BlockSpec((None, t, width), lambda b, c: (b, c, col // width))

    tab = pl.BlockSpec((t, LANES), lambda b, c: (c, 0))
    vec = pl.BlockSpec((1, LANES), lambda b, c: (0, 0))

    def outspec(width):
        return pl.BlockSpec((None, t, width), lambda b, c: (b, c, 0))

    return pl.pallas_call(
        _dsa_prep_kernel,
        grid=(bsz, s // t),
        in_specs=[colspec(COL_AQ, AT_WIDTH), colspec(COL_AKV, 2 * KV_WIDTH), colspec(COL_IQ, IDX_HEADS * IDX_DIM),
                  colspec(COL_SMALL, LANES), tab, tab, tab, vec, vec],
        out_specs=[pl.BlockSpec((None, AT_HEADS * LANES, t), lambda b, c: (b, 0, c)),
                   outspec(KV_WIDTH),
                   pl.BlockSpec((None, AT_KV_HEADS, VT_ROWS, t), lambda b, c: (b, 0, 0, c)),
                   pl.BlockSpec((None, IDX_HEADS * IDX_DIM, t), lambda b, c: (b, 0, c)),
                   outspec(LANES),
                   pl.BlockSpec((None, SUBLANES, t), lambda b, c: (b, 0, c))],
        out_shape=[jax.ShapeDtypeStruct((bsz, AT_HEADS * LANES, s), BF16),
                   jax.ShapeDtypeStruct((bsz, s, KV_WIDTH), BF16),
                   jax.ShapeDtypeStruct((bsz, AT_KV_HEADS, VT_ROWS, s), BF16),
                   jax.ShapeDtypeStruct((bsz, IDX_HEADS * IDX_DIM, s), BF16),
                   jax.ShapeDtypeStruct((bsz, s, LANES), BF16),
                   jax.ShapeDtypeStruct((bsz, SUBLANES, s), F32)],
        compiler_params=_cparams(2),
        name="dsa_prep",
    )(proj, proj, proj, proj, cos, sinp, sinm, ln_g, ln_b)


def _tree_sum(parts):
    while len(parts) > 1:
        parts = [parts[i] + parts[i + 1] for i in range(0, len(parts) - 1, 2)] + ([parts[-1]] if len(parts) % 2 else [])
    return parts[0]


def _dsa_kernel(qt_ref, iqt_ref, iwt_ref, k_ref, vt_ref, ik_ref, out_ref, score, coarse, bias, acc, m_s, x_s,
                *, topk, seq, ck):
    tq = qt_ref.shape[1]
    j = pl.program_id(1)
    nck = (j * tq + tq + ck - 1) // ck
    t_abs = j * tq + lax.broadcasted_iota(I32, (ck, tq), 1)
    s_loc = lax.broadcasted_iota(I32, (ck, tq), 0)
    rep = AT_HEADS // AT_KV_HEADS
    pack = 2 * SUBLANES

    iqt = iqt_ref[...]
    iwt = iwt_ref[...]
    iq_all = jnp.concatenate([iqt[h * IDX_DIM:(h + 1) * IDX_DIM, :] for h in range(IDX_HEADS)], axis=1)
    iw_all = jnp.concatenate([iwt[h:h + 1, :] for h in range(IDX_HEADS)], axis=1)

    def chunk_off(c):
        return pl.multiple_of(c * ck, ck)

    def score_body(c, carry):
        off = chunk_off(c)
        kic = ik_ref[pl.ds(off, ck), :][:, :IDX_DIM]
        sw = jnp.maximum(_dot(kic, iq_all), 0.0) * iw_all
        s = sw[:, :tq]
        for h in range(1, IDX_HEADS):
            s = s + sw[:, h * tq:(h + 1) * tq]
        s = jnp.where(off + s_loc <= t_abs, s, -jnp.inf)
        s = jnp.where(s == 0.0, 0.0, s)
        score[pl.ds(off, ck), :] = s
        coarse[pl.ds(off, ck), :] = s.astype(BF16)
        return carry

    lax.fori_loop(0, nck, score_body, 0)

    def count(pred):
        def body(c, a):
            off = chunk_off(c)
            m = jnp.where(pred(score[pl.ds(off, ck), :], off), 1, 0)
            return a + jnp.sum(m.reshape(ck // SUBLANES, SUBLANES, tq), axis=0)
        a = lax.fori_loop(0, nck, body, jnp.zeros((SUBLANES, tq), I32))
        return jnp.sum(a, axis=0, keepdims=True)

    def count_coarse(pred):
        one = jnp.ones((), BF16)
        zero = jnp.zeros((), BF16)

        def body(c, a):
            m = jnp.where(pred(coarse[pl.ds(chunk_off(c), ck), :]), one, zero).reshape(ck // pack, pack, tq)
            return a + _tree_sum([m[i] for i in range(ck // pack)]).astype(F32)
        a = lax.fori_loop(0, nck, body, jnp.zeros((pack, tq), F32))
        return jnp.sum(a, axis=0, keepdims=True).astype(I32)

    def key_to_f32(k):
        return pltpu.bitcast(k ^ ((k >> 31) & jnp.int32(0x7FFFFFFF)), F32)

    def key16_to_bf16(k):
        pattern = (k ^ ((k >> 15) & 0x7FFF)) & 0xFFFF
        return pltpu.bitcast(pattern << 16, F32).astype(BF16)

    lo16 = -(1 << 15)
    t16 = jnp.full((1, tq), lo16, I32)
    for bit in reversed(range(16)):
        cand = t16 + (1 << bit)
        cand_b = key16_to_bf16(cand)
        c = count_coarse(lambda hc, cand_b=cand_b: hc >= cand_b)
        t16 = jnp.where(c >= topk, cand, t16)
    key_lo = jnp.clip(t16 - jnp.where(t16 >= 0, 1, 0), lo16, BF16_INF_PATTERN - 1) << 16
    offs = jnp.zeros((1, tq), I32)
    for bit in reversed(range(17)):
        cand = offs + (1 << bit)
        cand_f = key_to_f32(key_lo + cand)
        c = count(lambda sc, off, cand_f=cand_f: sc >= cand_f)
        offs = jnp.where(c >= topk, cand, offs)
    thr = key_to_f32(key_lo + offs)

    take_all = t_abs[0:1, :] < topk
    n_gt = count(lambda sc, off: sc > thr)
    n_eq = count(lambda sc, off: sc == thr)
    need = topk - n_gt
    excess = (n_eq > need) & jnp.logical_not(take_all)
    x_s[...] = jnp.full(x_s.shape, seq, I32)

    any_excess = jnp.max(jnp.where(excess, 1, 0)) > 0

    @pl.when(any_excess)
    def _():
        x = jnp.zeros((1, tq), I32)
        for bit in reversed(range(max(seq - 1, 1).bit_length())):
            cand = x + (1 << bit)
            c = count(lambda sc, off: (sc == thr) & (off + s_loc < cand))
            x = jnp.where(c < need, cand, x)
        x_s[...] = jnp.broadcast_to(jnp.where(excess, x, seq), x_s.shape)

    def write_bias(with_ties):
        xlim = x_s[0:1, :]

        def bias_body(c, carry):
            off = chunk_off(c)
            sc = score[pl.ds(off, ck), :]
            s_abs = off + s_loc
            if with_ties:
                sel = (sc > thr) | ((sc == thr) & (s_abs <= xlim)) | take_all
            else:
                sel = (sc >= thr) | take_all
            bias[pl.ds(off, ck), :] = jnp.where(sel & (s_abs <= t_abs), 0.0, NEG_BIG)
            return carry

        lax.fori_loop(0, nck, bias_body, 0)

    pl.when(any_excess)(functools.partial(write_bias, True))
    pl.when(jnp.logical_not(any_excess))(functools.partial(write_bias, False))

    m_s[...] = jnp.full(m_s.shape, NEG_BIG, F32)
    acc[...] = jnp.zeros_like(acc)

    def att_body(c, carry):
        off = chunk_off(c)
        kc = k_ref[pl.ds(off, ck), :]
        bc = bias[pl.ds(off, ck), :]
        b_all = jnp.concatenate([bc] * rep, axis=1)
        for g in range(AT_KV_HEADS):
            qg = jnp.concatenate([qt_ref[(g * rep + r) * LANES:(g * rep + r + 1) * LANES, :] for r in range(rep)],
                                 axis=1)
            lg = _dot(kc, qg) + b_all
            m_old = m_s[g]
            m_new = jnp.maximum(m_old, jnp.max(lg, axis=0, keepdims=True))
            a = jnp.exp2(m_old - m_new)
            p = jnp.exp2(lg - m_new)
            acc[g] = a * acc[g] + _dot(vt_ref[g, :, pl.ds(off, ck)], p.astype(BF16))
            m_s[g] = m_new
        return carry

    lax.fori_loop(0, nck, att_body, 0)

    for g in range(AT_KV_HEADS):
        ag = acc[g]
        on = ag[:AT_DIM] / ag[AT_DIM:AT_DIM + 1]
        for pp in range(rep // 2):
            two = jnp.concatenate([on[:, 2 * pp * tq:(2 * pp + 1) * tq], on[:, (2 * pp + 1) * tq:(2 * pp + 2) * tq]],
                                  axis=0)
            pair = g * (rep // 2) + pp
            out_ref[:, pair * LANES:(pair + 1) * LANES] = two.T.astype(out_ref.dtype)


def _dsa_attention(q_t, k_r, v_t, iq_t, ik_r, iw_t):
    bsz, s, _ = k_r.shape
    tq = min(512, s)
    topk = min(TOPK_MAX, s // 4)
    rep = AT_HEADS // AT_KV_HEADS
    ck = min(512, s)
    assert s % ck == 0 and ck >= topk and ck % tq == 0
    kern = functools.partial(_dsa_kernel, topk=topk, seq=s, ck=ck)
    return pl.pallas_call(
        kern,
        grid=(bsz, s // tq),
        in_specs=[pl.BlockSpec((None, AT_HEADS * LANES, tq), lambda b, j: (b, 0, j)),
                  pl.BlockSpec((None, IDX_HEADS * IDX_DIM, tq), lambda b, j: (b, 0, j)),
                  pl.BlockSpec((None, SUBLANES, tq), lambda b, j: (b, 0, j)),
                  pl.BlockSpec((None, s, KV_WIDTH), lambda b, j: (b, 0, 0)),
                  pl.BlockSpec((None, AT_KV_HEADS, VT_ROWS, s), lambda b, j: (b, 0, 0, 0)),
                  pl.BlockSpec((None, s, LANES), lambda b, j: (b, 0, 0))],
        out_specs=pl.BlockSpec((None, tq, AT_WIDTH), lambda b, j: (b, j, 0)),
        out_shape=jax.ShapeDtypeStruct((bsz, s, AT_WIDTH), BF16),
        scratch_shapes=[pltpu.VMEM((s, tq), F32),
                        pltpu.VMEM((s, tq), BF16),
                        pltpu.VMEM((s, tq), F32),
                        pltpu.VMEM((AT_KV_HEADS, VT_ROWS, rep * tq), F32),
                        pltpu.VMEM((AT_KV_HEADS, 1, rep * tq), F32),
                        pltpu.VMEM((SUBLANES, tq), I32)],
        compiler_params=_cparams(2),
        name="dsa_attn",
    )(q_t, iq_t, iw_t, k_r, v_t, ik_r)


def _out_ln_kernel(ml_ref, at_ref, x_ref, wt_ref, wb_ref, g_ref, b_ref, o_ref, *, alpha):
    mix = _dot(ml_ref[...], wt_ref[...]) + _dot(at_ref[...], wb_ref[...])
    o_ref[...] = _layer_norm(alpha * x_ref[...] + mix, g_ref[...], b_ref[...])


def _out_proj_ln(ml, at, x2d, w_top, w_bot, g, b, alpha):
    n, d = x2d.shape
    tm = 512
    row = lambda i: (i, 0)
    const = lambda i: (0, 0)
    return pl.pallas_call(
        functools.partial(_out_ln_kernel, alpha=alpha),
        grid=(n // tm,),
        in_specs=[pl.BlockSpec((tm, ML_WIDTH), row), pl.BlockSpec((tm, AT_WIDTH), row), pl.BlockSpec((tm, d), row),
                  pl.BlockSpec((ML_WIDTH, d), const), pl.BlockSpec((AT_WIDTH, d), const),
                  pl.BlockSpec((1, d), const), pl.BlockSpec((1, d), const)],
        out_specs=pl.BlockSpec((tm, d), row),
        out_shape=jax.ShapeDtypeStruct((n, d), F32),
        compiler_params=_cparams(1),
        name="out_proj_ln",
    )(ml, at, x2d, w_top, w_bot, g, b)


def _ffn_kernel(x_ref, wg_ref, wu_ref, wd_ref, g_ref, b_ref, o_ref, acc_ref, xb_ref, *, alpha):
    f = pl.program_id(1)

    @pl.when(f == 0)
    def _():
        xb_ref[...] = x_ref[...].astype(BF16)
        acc_ref[...] = jnp.zeros_like(acc_ref)

    xb = xb_ref[...]
    h = _silu(_dot(xb, wg_ref[...])) * _dot(xb, wu_ref[...])
    acc_ref[...] += _dot(h.astype(BF16), wd_ref[...])

    @pl.when(f == pl.num_programs(1) - 1)
    def _():
        o_ref[...] = _layer_norm(alpha * x_ref[...] + acc_ref[...], g_ref[...], b_ref[...])


def _ffn_chunk(d_ff):
    best = LANES
    for c in range(LANES, 1408 + 1, LANES):
        if d_ff % c == 0:
            best = c
    return best


def _dense_ffn_ln(x2d, wg, wu, wd, g, b, alpha):
    n, d = x2d.shape
    d_ff = wg.shape[1]
    tm = 512
    fc = _ffn_chunk(d_ff)
    return pl.pallas_call(
        functools.partial(_ffn_kernel, alpha=alpha),
        grid=(n // tm, d_ff // fc),
        in_specs=[pl.BlockSpec((tm, d), lambda i, f: (i, 0)),
                  pl.BlockSpec((d, fc), lambda i, f: (0, f)),
                  pl.BlockSpec((d, fc), lambda i, f: (0, f)),
                  pl.BlockSpec((fc, d), lambda i, f: (f, 0)),
                  pl.BlockSpec((1, d), lambda i, f: (0, 0)),
                  pl.BlockSpec((1, d), lambda i, f: (0, 0))],
        out_specs=pl.BlockSpec((tm, d), lambda i, f: (i, 0)),
        out_shape=jax.ShapeDtypeStruct((n, d), F32),
        scratch_shapes=[pltpu.VMEM((tm, d), F32), pltpu.VMEM((tm, d), BF16)],
        compiler_params=_cparams(2),
        name="dense_ffn_ln",
    )(x2d, wg, wu, wd, g, b)


def _router_kernel(x_ref, wh_ref, wl_ref, br_ref, comb_ref, dest_ref, cnt_ref):
    tb = x_ref.shape[0]
    x = x_ref[...]
    xh = x.astype(BF16)
    xl = (x - xh.astype(F32)).astype(BF16)
    wh = wh_ref[...]
    lg = (lax.dot_general(wh, xh, NT_DIMS, preferred_element_type=F32)
          + lax.dot_general(wh, xl, NT_DIMS, preferred_element_type=F32)
          + lax.dot_general(wl_ref[...], xh, NT_DIMS, preferred_element_type=F32)) + br_ref[:, 0:1]
    e_id = lax.broadcasted_iota(I32, (N_EXPERTS, tb), 0)
    m1 = jnp.max(lg, axis=0, keepdims=True)
    i1 = jnp.min(jnp.where(lg == m1, e_id, N_EXPERTS), axis=0, keepdims=True)
    lg2 = jnp.where(e_id == i1, -jnp.inf, lg)
    m2 = jnp.max(lg2, axis=0, keepdims=True)
    i2 = jnp.min(jnp.where(lg2 == m2, e_id, N_EXPERTS), axis=0, keepdims=True)
    ex = jnp.exp(m2 - m1)
    g1 = 1.0 / (1.0 + ex)
    g2 = ex / (1.0 + ex)
    comb = jnp.where(e_id == i1, g1, 0.0) + jnp.where(e_id == i2, g2, 0.0)
    comb_ref[...] = comb
    mask = jnp.where((e_id == i1) | (e_id == i2), 1.0, 0.0)
    w = MOE_TOKEN_CHUNK
    upper = jnp.where(lax.broadcasted_iota(I32, (w, w), 0) < lax.broadcasted_iota(I32, (w, w), 1), 1.0, 0.0).astype(BF16)
    lane = lax.broadcasted_iota(I32, (N_EXPERTS, LANES), 1)
    cnt = jnp.zeros((N_EXPERTS, LANES), F32)
    off = jnp.zeros((N_EXPERTS, 1), F32)
    for c in range(tb // w):
        cnt = jnp.where(lane == c, off, cnt)
        mc = mask[:, c * w:(c + 1) * w]
        dest_ref[:, c * w:(c + 1) * w] = (_dot(mc.astype(BF16), upper) + off).astype(I32)
        off = off + jnp.sum(mc, axis=1, keepdims=True)
    cnt_ref[...] = jnp.where(lane == tb // w, off, cnt).astype(I32)


def _router(x2d, w_hi, w_lo, b_r, tb):
    n, d = x2d.shape
    nb = n // tb
    return pl.pallas_call(
        _router_kernel,
        grid=(nb,),
        in_specs=[pl.BlockSpec((tb, d), lambda i: (i, 0)),
                  pl.BlockSpec((N_EXPERTS, d), lambda i: (0, 0)),
                  pl.BlockSpec((N_EXPERTS, d), lambda i: (0, 0)),
                  pl.BlockSpec((N_EXPERTS, LANES), lambda i: (0, 0))],
        out_specs=[pl.BlockSpec((N_EXPERTS, tb), lambda i: (0, i)),
                   pl.BlockSpec((N_EXPERTS, tb), lambda i: (0, i)),
                   pl.BlockSpec((None, N_EXPERTS, LANES), lambda i: (i, 0, 0))],
        out_shape=[jax.ShapeDtypeStruct((N_EXPERTS, n), F32),
                   jax.ShapeDtypeStruct((N_EXPERTS, n), I32),
                   jax.ShapeDtypeStruct((nb, N_EXPERTS, LANES), I32)],
        compiler_params=_cparams(1),
        name="moe_router",
    )(x2d, w_hi, w_lo, b_r)


def _moe_kernel(cnt_ref, x_ref, dest_ref, comb_ref, wg_ref, wu_ref, wd_ref, g_ref, b_ref, o_ref,
                xb, xe, ye, gs, *, alpha):
    sub = MOE_ROW_TILE
    cw = MOE_TOKEN_CHUNK
    tb, d = x_ref.shape
    nch = tb // cw
    i = pl.program_id(0)
    e = pl.program_id(1)
    f = pl.program_id(2)
    n_e = pl.num_programs(1)
    n_f = pl.num_programs(2)
    base = (i * n_e + e) * (nch + 1)
    cum = [cnt_ref[base + k] for k in range(nch + 1)]
    nsub = (cum[nch] + sub - 1) // sub

    @pl.when((e == 0) & (f == 0))
    def _():
        xb[...] = x_ref[...].astype(BF16)
        o_ref[...] = jnp.zeros_like(o_ref)

    def onehot(r0, ch):
        cols = slice(ch * cw, (ch + 1) * cw)
        drow = dest_ref[pl.ds(e, 1), cols]
        crow = comb_ref[pl.ds(e, 1), cols]
        slot = r0 + lax.broadcasted_iota(I32, (sub, cw), 0)
        return jnp.where((drow == slot) & (crow > 0.0), 1.0, 0.0).astype(BF16)

    def for_overlapping_chunks(r0, fn):
        for ch in range(nch):
            pl.when((cum[ch] < r0 + sub) & (cum[ch + 1] > r0))(functools.partial(fn, ch))

    @pl.when(f == 0)
    def _():
        def body(r, carry):
            r0 = pl.multiple_of(r * sub, sub)
            rows = pl.ds(r0, sub)
            xe[rows, :] = jnp.zeros((sub, d), BF16)
            ye[rows, :] = jnp.zeros((sub, d), F32)
            gs[rows, :] = jnp.zeros((sub, LANES), F32)

            def gather(ch):
                cols = slice(ch * cw, (ch + 1) * cw)
                p = onehot(r0, ch)
                xe[rows, :] += _dot(p, xb[cols, :]).astype(BF16)
                crow = comb_ref[pl.ds(e, 1), cols]
                hi = crow.astype(BF16)
                lo = (crow - hi.astype(F32)).astype(BF16)
                c2 = jnp.concatenate([hi, lo, jnp.zeros((LANES - 2, cw), BF16)], axis=0)
                gs[rows, :] += lax.dot_general(p, c2, NT_DIMS, preferred_element_type=F32)

            for_overlapping_chunks(r0, gather)
            return carry
        lax.fori_loop(0, nsub, body, 0)

    def ffn_tile(r0, m):
        rows = pl.ds(r0, m)
        xr = xe[rows, :]
        h = _silu(_dot(xr, wg_ref[...])) * _dot(xr, wu_ref[...])
        ye[rows, :] += _dot(h.astype(BF16), wd_ref[...])

    def ffn_quad(r, carry):
        ffn_tile(pl.multiple_of(r * 4 * sub, 4 * sub), 4 * sub)
        return carry

    n_quad = nsub // 4
    lax.fori_loop(0, n_quad, ffn_quad, 0)
    rest = nsub - 4 * n_quad
    rest0 = n_quad * 4 * sub

    @pl.when(rest >= 2)
    def _():
        ffn_tile(pl.multiple_of(rest0, 2 * sub), 2 * sub)

    @pl.when(rest % 2 == 1)
    def _():
        ffn_tile(pl.multiple_of(rest0 + (rest // 2) * 2 * sub, sub), sub)

    @pl.when(f == n_f - 1)
    def _():
        def body(r, carry):
            r0 = pl.multiple_of(r * sub, sub)
            rows = pl.ds(r0, sub)
            gate = gs[rows, 0:1] + gs[rows, 1:2]
            yw = (ye[rows, :] * gate).astype(BF16)

            def scatter(ch):
                cols = slice(ch * cw, (ch + 1) * cw)
                o_ref[cols, :] += lax.dot_general(onehot(r0, ch), yw, TN_DIMS, preferred_element_type=F32)

            for_overlapping_chunks(r0, scatter)
            return carry
        lax.fori_loop(0, nsub, body, 0)

    @pl.when((e == n_e - 1) & (f == n_f - 1))
    def _():
        o_ref[...] = _layer_norm(alpha * x_ref[...] + o_ref[...], g_ref[...], b_ref[...])


def _moe_ffn_ln(x2d, counts, dest_t, comb_t, wg, wu, wd, g, b, alpha, tb):
    n, d = x2d.shape
    n_e, _, d_ff = wg.shape
    fc = 896
    grid_spec = pltpu.PrefetchScalarGridSpec(
        num_scalar_prefetch=1,
        grid=(n // tb, n_e, d_ff // fc),
        in_specs=[pl.BlockSpec((tb, d), lambda i, e, f, cnt: (i, 0), pipeline_mode=pl.Buffered(1)),
                  pl.BlockSpec((n_e, tb), lambda i, e, f, cnt: (0, i)),
                  pl.BlockSpec((n_e, tb), lambda i, e, f, cnt: (0, i)),
                  pl.BlockSpec((None, d, fc), lambda i, e, f, cnt: (e, 0, f)),
                  pl.BlockSpec((None, d, fc), lambda i, e, f, cnt: (e, 0, f)),
                  pl.BlockSpec((None, fc, d), lambda i, e, f, cnt: (e, f, 0)),
                  pl.BlockSpec((1, d), lambda i, e, f, cnt: (0, 0)),
                  pl.BlockSpec((1, d), lambda i, e, f, cnt: (0, 0))],
        out_specs=pl.BlockSpec((tb, d), lambda i, e, f, cnt: (i, 0), pipeline_mode=pl.Buffered(1)),
        scratch_shapes=[pltpu.VMEM((tb, d), BF16), pltpu.VMEM((tb, d), BF16), pltpu.VMEM((tb, d), F32),
                        pltpu.VMEM((tb, LANES), F32)],
    )
    return pl.pallas_call(
        functools.partial(_moe_kernel, alpha=alpha),
        grid_spec=grid_spec,
        out_shape=jax.ShapeDtypeStruct((n, d), F32),
        compiler_params=_cparams(3),
        name="moe_ffn_ln",
    )(counts, x2d, dest_t, comb_t, wg, wu, wd, g, b)


def _rope_tables(s):
    half = AT_DIM // 8
    inv = ROPE_THETA ** (-jnp.arange(half, dtype=F32) / half)
    ang = jnp.arange(s).astype(F32)[:, None] * inv[None, :]
    cos = jnp.cos(ang)
    sin = jnp.sin(ang)
    ones = jnp.ones((s, AT_DIM - 2 * half), F32)
    zeros = jnp.zeros((s, AT_DIM - 2 * half), F32)
    z8 = jnp.zeros((s, half), F32)
    cos64 = jnp.concatenate([cos, cos, ones], axis=1)
    sinp64 = jnp.concatenate([z8, sin, zeros], axis=1)
    sinm64 = jnp.concatenate([-sin, z8, zeros], axis=1)
    rep = LANES // AT_DIM
    return (jnp.tile(cos64, (1, rep)), jnp.tile(sinp64, (1, rep)), jnp.tile(sinm64, (1, rep)))


def _permute_w_in(w):
    n_in = w.shape[1]
    n_ml = 4 * ML_WIDTH
    n_gate = 2 * ML_HEADS
    src = np.full((PROJ_PAD,), -1, np.int32)
    src[:n_ml] = np.arange(n_ml)
    src[n_ml:n_in - n_gate] = np.arange(n_ml + n_gate, n_in)
    src[n_in - n_gate:n_in] = np.arange(n_ml, n_ml + n_gate)
    sel = (jnp.arange(n_in, dtype=I32)[:, None] == jnp.asarray(src)[None, :]).astype(BF16)
    return jnp.dot(w.astype(BF16), sel, preferred_element_type=F32).astype(BF16)


def kernel(x, w_in, ml_conv_w, ml_conv_b, ml_i_b, ml_f_b, ml_norm_g, idx_k_norm_g, idx_k_norm_b, w_out, ln1_g, ln1_b, ln2_g, ln2_b, ffn_w_gate, ffn_w_up, ffn_w_down, moe_w_router, moe_b_router, moe_w_gate, moe_w_up, moe_w_down):
    bsz, s, d = x.shape
    depth = w_in.shape[0]
    alpha = float((2 * depth) ** 0.25)
    n = bsz * s
    cos, sinp, sinm = _rope_tables(s)
    ml_chunk = min(256, s)
    moe_tb = min(2048, n)
    zpad = lambda v, left: jnp.pad(v, (left, LANES - left - v.shape[0]))[None, :]

    x2d = x.reshape(n, d)
    for l in range(depth):
        proj = _in_proj(x2d, _permute_w_in(w_in[l]).astype(BF16)).reshape(bsz, s, PROJ_PAD)
        gate_b = zpad(jnp.concatenate([ml_i_b[l], ml_f_b[l]]), SM_MI)
        ml_out = _mlstm(proj, ml_conv_w[l], ml_conv_b[l][None, :], gate_b, ml_norm_g[l][None, :], ml_chunk)
        q_r, k_r, v_t, iq_r, ik_r, iw_t = _dsa_prep(proj, cos, sinp, sinm,
                                                    zpad(idx_k_norm_g[l], 0), zpad(idx_k_norm_b[l], 0))
        at_out = _dsa_attention(q_r, k_r, v_t, iq_r, ik_r, iw_t)
        wo = w_out[l].astype(BF16)
        x2d = _out_proj_ln(ml_out.reshape(n, ML_WIDTH), at_out.reshape(n, AT_WIDTH), x2d,
                           wo[:ML_WIDTH], wo[ML_WIDTH:], ln1_g[l][None, :], ln1_b[l][None, :], alpha)
        j = l // 2
        if l % 2 == 0:
            x2d = _dense_ffn_ln(x2d, ffn_w_gate[j].astype(BF16), ffn_w_up[j].astype(BF16),
                                ffn_w_down[j].astype(BF16), ln2_g[l][None, :], ln2_b[l][None, :], alpha)
        else:
            wr_t = moe_w_router[j].T
            wr_hi = wr_t.astype(BF16)
            wr_lo = (wr_t - wr_hi.astype(F32)).astype(BF16)
            br = jnp.broadcast_to(moe_b_router[j][:, None], (N_EXPERTS, LANES))
            comb_t, dest_t, cnt = _router(x2d, wr_hi, wr_lo, br, moe_tb)
            counts = cnt[:, :, :moe_tb // MOE_TOKEN_CHUNK + 1].reshape(-1)
            x2d = _moe_ffn_ln(x2d, counts, dest_t, comb_t, moe_w_gate[j].astype(BF16), moe_w_up[j].astype(BF16),
                              moe_w_down[j].astype(BF16), ln2_g[l][None, :], ln2_b[l][None, :], alpha, moe_tb)
    return x2d.reshape(bsz, s, d)
```
